```python
import jax, jax.numpy as jnp
from jax import lax
import numpy as np

D_MODEL = 2048
BATCH = 8
SEQ = 2048
DEPTH = 1
DEC_BATCH = 32
DEC_SEQ = 4
PAST_LEN = 8192
PAGE_SIZE = 128

MIX_WIDTH = D_MODEL
GLA_WIDTH = MIX_WIDTH // 2
GLA_HEADS = 4
GLA_DV = GLA_WIDTH // GLA_HEADS
GLA_DK = GLA_DV // 2
GLA_GATE_RANK = 16
GLA_TAU = 16.0
GLA_CHUNK = 64
DSWA_WIDTH = MIX_WIDTH - GLA_WIDTH
DSWA_HEAD_DIM = 128
DSWA_HEADS = DSWA_WIDTH // DSWA_HEAD_DIM
DSWA_BRANCHES = ((128, 1), (512, 4), (2048, 16))
DSWA_WINDOW = 2048
N_EXPERTS = 32
TOP_K = 4
D_FF = D_MODEL
SWIGLU_LIMIT = 7.0
SWIGLU_ALPHA = 1.702
MOE_BLOCK = 128
DEEPNORM_ALPHA = (2.0 * DEPTH) ** 0.25
DEEPNORM_BETA = (8.0 * DEPTH) ** -0.25
LN_EPS = 1e-5
RMS_EPS = 1e-6
NEG_BIG = -1e30
IN_SIZES = (GLA_HEADS * GLA_DK, GLA_HEADS * GLA_DK, GLA_WIDTH, GLA_GATE_RANK, GLA_WIDTH,
            DSWA_WIDTH, DSWA_WIDTH, DSWA_WIDTH)
IN_SPLITS = tuple(int(c) for c in np.cumsum(IN_SIZES)[:-1])
D_IN = int(sum(IN_SIZES))

kernel_name = 'hymba_gla_dilated_swa_moe_step'


def _layer_norm(x, g, b):
    xf = x.astype(jnp.float32)
    mu = xf.mean(-1, keepdims=True)
    var = jnp.square(xf - mu).mean(-1, keepdims=True)
    y = (xf - mu) * lax.rsqrt(var + LN_EPS) * g.astype(jnp.float32) + b.astype(jnp.float32)
    return y.astype(x.dtype)


def _alibi_slopes():
    return 2.0 ** (-8.0 * (jnp.arange(DSWA_HEADS, dtype=jnp.float32) + 1.0) / DSWA_HEADS)


def _project(x, w_in, w_gla_alpha, b_gla_alpha):
    B, L, _ = x.shape
    gq, gk, gv, glr, gr, dq, dk, dv = jnp.split(x @ w_in, IN_SPLITS, axis=-1)
    log_a = jax.nn.log_sigmoid((glr @ w_gla_alpha + b_gla_alpha).astype(jnp.float32)) / GLA_TAU
    gh = lambda a: a.reshape(B, L, GLA_HEADS, -1)
    dh = lambda a: a.reshape(B, L, DSWA_HEADS, DSWA_HEAD_DIM)
    return gh(gq), gh(gk), gh(gv), gh(log_a), gr, dh(dq), dh(dk), dh(dv)


def _gla(q, k, v, log_a, s0):
    B, L, H, DK = q.shape
    DV = v.shape[-1]
    C = GLA_CHUNK if L % GLA_CHUNK == 0 else L
    nc = L // C

    def chunks(a):
        return a.astype(jnp.float32).reshape(B, nc, C, H, a.shape[-1]).transpose(1, 0, 3, 2, 4)

    causal = jnp.tril(jnp.ones((C, C), dtype=bool))[:, :, None]

    def step(S, inp):
        qc, kc, vc, gc = inp
        b = jnp.cumsum(gc, axis=2)
        o = jnp.einsum('bhtk,bhkv->bhtv', qc * jnp.exp(b), S)
        decay = jnp.exp(jnp.where(causal, b[:, :, :, None, :] - b[:, :, None, :, :], -jnp.inf))
        att = jnp.einsum('bhtsk,bhsk->bhts', qc[:, :, :, None, :] * decay, kc)
        o = o + jnp.einsum('bhts,bhsv->bhtv', att, vc)
        b_end = b[:, :, -1]
        S = jnp.exp(b_end)[..., None] * S + jnp.einsum(
            'bhsk,bhsv->bhkv', kc * jnp.exp(b_end[:, :, None] - b), vc)
        return S, o

    s_final, o = lax.scan(step, s0.astype(jnp.float32),
                          (chunks(q) * (GLA_DK ** -0.5), chunks(k), chunks(v), chunks(log_a)))
    o = o.transpose(1, 0, 3, 2, 4).reshape(B, L, H, DV)
    return o, s_final


def _gla_output(o, r, gla_norm_w):
    B, L = o.shape[:2]
    on = o * lax.rsqrt(jnp.mean(o * o, -1, keepdims=True) + RMS_EPS)
    on = on.reshape(B, L, GLA_WIDTH) * gla_norm_w.astype(jnp.float32)
    return (on * jax.nn.silu(r.astype(jnp.float32))).astype(r.dtype)


def _softmax_stats(s):
    mx = s.max(-1, keepdims=True)
    p = jnp.exp(s - mx)
    den = p.sum(-1, keepdims=True)
    return p / den, (mx + jnp.log(den))[..., 0]


def _dswa_prompt_branch(q, k, v, window, dil, slopes):
    B, S, H, E = q.shape
    n = window // dil
    M = -(-(S + n * dil) // dil)
    nb = -(-M // n)
    T = nb * n * dil
    pad = ((0, 0), (n * dil, T - S - n * dil), (0, 0), (0, 0))
    blocks = lambda a: jnp.pad(a, pad).reshape(B, nb, n, dil, H, E)
    qb = blocks(q)[:, 1:]
    kb, vb = blocks(k), blocks(v)
    kw = jnp.concatenate([kb[:, :-1], kb[:, 1:]], axis=2)
    vw = jnp.concatenate([vb[:, :-1], vb[:, 1:]], axis=2)
    s = jnp.einsum('bnqrhe,bnkrhe->bnrhqk', qb, kw,
                   preferred_element_type=jnp.float32) * (E ** -0.5)
    dist = n + jnp.arange(n)[:, None] - jnp.arange(2 * n)[None, :]
    blk = jnp.arange(nb - 1)[:, None, None]
    valid = (dist >= 0) & (dist <= n) & (blk * n + jnp.arange(2 * n)[None, None, :] >= n)
    s = s - slopes[:, None, None] * (dist * dil).astype(jnp.float32)
    s = jnp.where(valid[None, :, None, None], s, NEG_BIG)
    p, lse = _softmax_stats(s)
    o = jnp.einsum('bnrhqk,bnkrhe->bnqrhe', p, vw.astype(jnp.float32))
    o = o.reshape(B, -1, H, E)[:, :S]
    lse = lse.transpose(0, 1, 4, 2, 3).reshape(B, -1, H)[:, :S]
    return o, lse


def _dswa_sample_branch(q, k_all, v_all, window, dil, slopes):
    B, L, H, E = q.shape
    n_past = k_all.shape[1] - L
    n = window // dil
    steps = jnp.arange(n + 1)
    idx = n_past + jnp.arange(L)[:, None] - steps[None, :] * dil
    valid = idx >= 0
    idx = jnp.maximum(idx, 0)
    kg = k_all[:, idx]
    vg = v_all[:, idx]
    s = jnp.einsum('bqhe,bqjhe->bhqj', q, kg, preferred_element_type=jnp.float32) * (E ** -0.5)
    s = s - slopes[:, None, None] * (steps * dil).astype(jnp.float32)
    s = jnp.where(valid, s, NEG_BIG)
    p, lse = _softmax_stats(s)
    o = jnp.einsum('bhqj,bqjhe->bqhe', p, vg.astype(jnp.float32))
    return o, lse.transpose(0, 2, 1)


def _merge_branches(branches):
    o = jnp.stack([br[0] for br in branches])
    w = jax.nn.softmax(jnp.stack([br[1] for br in branches]), axis=0)
    return jnp.sum(w[..., None] * o, axis=0)


def _moe(h, w_router, b_router, w_exp_gate, b_exp_gate, w_exp_up, b_exp_up, w_exp_down, b_exp_down):
    B, L, D = h.shape
    n_tok = B * L
    n_assign = n_tok * TOP_K
    hf = h.reshape(n_tok, D)
    logits = jnp.matmul(hf, w_router, preferred_element_type=jnp.float32) + b_router.astype(jnp.float32)
    top_logit, top_e = lax.top_k(logits, TOP_K)
    gate = jax.nn.softmax(top_logit, axis=-1)
    e_flat = top_e.reshape(-1)
    tok_flat = jnp.repeat(jnp.arange(n_tok, dtype=jnp.int32), TOP_K)
    order = jnp.argsort(e_flat)
    e_sorted, tok_sorted, gate_sorted = e_flat[order], tok_flat[order], gate.reshape(-1)[order]
    counts = jnp.bincount(e_flat, length=N_EXPERTS)
    padded = (counts + MOE_BLOCK - 1) // MOE_BLOCK * MOE_BLOCK
    start = jnp.cumsum(counts) - counts
    pend = jnp.cumsum(padded)
    pstart = pend - padded
    dest = pstart[e_sorted] + jnp.arange(n_assign) - start[e_sorted]
    n_blocks = (n_assign + N_EXPERTS * (MOE_BLOCK - 1) + MOE_BLOCK - 1) // MOE_BLOCK
    n_rows = n_blocks * MOE_BLOCK
    row_tok = jnp.full((n_rows,), n_tok, jnp.int32).at[dest].set(tok_sorted)
    row_gate = jnp.zeros((n_rows,), jnp.float32).at[dest].set(gate_sorted)
    blk_e = jnp.minimum(jnp.searchsorted(pend, jnp.arange(n_blocks) * MOE_BLOCK, side='right'),
                        N_EXPERTS - 1)
    x_rows = jnp.concatenate([hf, jnp.zeros((1, D), hf.dtype)])[row_tok].reshape(n_blocks, MOE_BLOCK, D)

    def expert_block(args):
        xb, e = args
        g = jnp.minimum(xb @ w_exp_gate[e] + b_exp_gate[e], SWIGLU_LIMIT)
        u = jnp.clip(xb @ w_exp_up[e] + b_exp_up[e], -SWIGLU_LIMIT, SWIGLU_LIMIT)
        a = g * jax.nn.sigmoid(SWIGLU_ALPHA * g) * (u + 1.0)
        return a @ w_exp_down[e] + b_exp_down[e]

    y_rows = lax.map(expert_block, (x_rows, blk_e)).reshape(n_rows, D)
    y = jax.ops.segment_sum(y_rows.astype(jnp.float32) * row_gate[:, None], row_tok,
                            num_segments=n_tok + 1)[:n_tok]
    return y.astype(h.dtype).reshape(B, L, D)


def _finish(x, o_gla, o_dswa, w_o, ln1_g, ln1_b, w_router, b_router, w_exp_gate, b_exp_gate,
            w_exp_up, b_exp_up, w_exp_down, b_exp_down, ln2_g, ln2_b):
    B, L, _ = x.shape
    heads = jnp.concatenate([o_gla, o_dswa.reshape(B, L, DSWA_WIDTH).astype(o_gla.dtype)], axis=-1)
    h = _layer_norm(DEEPNORM_ALPHA * x + heads @ w_o, ln1_g, ln1_b)
    f = _moe(h, w_router, b_router, w_exp_gate, b_exp_gate, w_exp_up, b_exp_up, w_exp_down, b_exp_down)
    return _layer_norm(DEEPNORM_ALPHA * h + f, ln2_g, ln2_b)


def setup_inputs(seed: int = 0) -> dict:
    key = jax.random.key(seed)
    ks = jax.random.split(key, 24)
    f32 = jnp.float32
    nrm = lambda k, shape, s: jax.random.normal(k, shape, f32) * s
    sc = D_MODEL ** -0.5
    buf = min(DSWA_WINDOW, PAST_LEN)
    in_scale = np.concatenate([np.full((n,), s, np.float32) for n, s in zip(
        IN_SIZES, (1.0, 1.0, DEEPNORM_BETA, 1.0, 1.0, 1.0, 1.0, DEEPNORM_BETA))])
    return {
        'x_prompt': nrm(ks[0], (BATCH, SEQ, D_MODEL), 1.0),
        'x_sample': nrm(ks[1], (DEC_BATCH, DEC_SEQ, D_MODEL), 1.0),
        'state_gla': nrm(ks[2], (DEC_BATCH, GLA_HEADS, GLA_DK, GLA_DV), 0.5),
        'cache_dswa_k': nrm(ks[3], (DEC_BATCH, buf, DSWA_HEADS, DSWA_HEAD_DIM), 1.0),
        'cache_dswa_v': nrm(ks[4], (DEC_BATCH, buf, DSWA_HEADS, DSWA_HEAD_DIM), DEEPNORM_BETA),
        'w_in': nrm(ks[5], (D_MODEL, D_IN), sc) * jnp.asarray(in_scale),
        'w_gla_alpha': nrm(ks[6], (GLA_GATE_RANK, GLA_HEADS * GLA_DK), GLA_GATE_RANK ** -0.5),
        'b_gla_alpha': nrm(ks[7], (GLA_HEADS * GLA_DK,), 0.1),
        'gla_norm_w': 1.0 + nrm(ks[8], (GLA_WIDTH,), 0.02),
        'w_o': nrm(ks[9], (MIX_WIDTH, D_MODEL), MIX_WIDTH ** -0.5 * DEEPNORM_BETA),
        'ln1_g': 1.0 + nrm(ks[10], (D_MODEL,), 0.02),
        'ln1_b': nrm(ks[11], (D_MODEL,), 0.02),
        'w_router': nrm(ks[12], (D_MODEL, N_EXPERTS), sc),
        'b_router': nrm(ks[13], (N_EXPERTS,), 0.01),
        'w_exp_gate': nrm(ks[14], (N_EXPERTS, D_MODEL, D_FF), sc),
        'b_exp_gate': nrm(ks[15], (N_EXPERTS, D_FF), 0.01),
        'w_exp_up': nrm(ks[16], (N_EXPERTS, D_MODEL, D_FF), sc),
        'b_exp_up': nrm(ks[17], (N_EXPERTS, D_FF), 0.01),
        'w_exp_down': nrm(ks[18], (N_EXPERTS, D_FF, D_MODEL), D_FF ** -0.5 * DEEPNORM_BETA),
        'b_exp_down': nrm(ks[19], (N_EXPERTS, D_MODEL), 0.01),
        'ln2_g': 1.0 + nrm(ks[20], (D_MODEL,), 0.02),
        'ln2_b': nrm(ks[21], (D_MODEL,), 0.02),
    }


def reference(x_prompt, x_sample, state_gla, cache_dswa_k, cache_dswa_v, w_in, w_gla_alpha,
              b_gla_alpha, gla_norm_w, w_o, ln1_g, ln1_b, w_router, b_router, w_exp_gate,
              b_exp_gate, w_exp_up, b_exp_up, w_exp_down, b_exp_down, ln2_g, ln2_b):
    slopes = _alibi_slopes()
    tail = (w_o, ln1_g, ln1_b, w_router, b_router, w_exp_gate, b_exp_gate, w_exp_up, b_exp_up,
            w_exp_down, b_exp_down, ln2_g, ln2_b)

    for _ in range(DEPTH):
        B, S, _ = x_prompt.shape
        gq, gk, gv, ga, gr, dq, dk, dv = _project(x_prompt, w_in, w_gla_alpha, b_gla_alpha)
        o_gla, state_gla_prompt = _gla(gq, gk, gv, ga,
                                       jnp.zeros((B, GLA_HEADS, GLA_DK, GLA_DV), jnp.float32))
        o_gla = _gla_output(o_gla, gr, gla_norm_w)
        o_dswa = _merge_branches([_dswa_prompt_branch(dq, dk, dv, w, d, slopes)
                                  for (w, d) in DSWA_BRANCHES])
        y_prompt = _finish(x_prompt, o_gla, o_dswa, *tail)
        keep = min(DSWA_WINDOW, S)
        cache_dswa_k_prompt = dk[:, S - keep:]
        cache_dswa_v_prompt = dv[:, S - keep:]

        sq, sk, sv, sa, sr, tq, tk, tv = _project(x_sample, w_in, w_gla_alpha, b_gla_alpha)
        o_gla_s, state_gla_sample = _gla(sq, sk, sv, sa, state_gla)
        o_gla_s = _gla_output(o_gla_s, sr, gla_norm_w)
        k_all = jnp.concatenate([cache_dswa_k.astype(tk.dtype), tk], axis=1)
        v_all = jnp.concatenate([cache_dswa_v.astype(tv.dtype), tv], axis=1)
        o_dswa_s = _merge_branches([_dswa_sample_branch(tq, k_all, v_all, w, d, slopes)
                                    for (w, d) in DSWA_BRANCHES])
        y_sample = _finish(x_sample, o_gla_s, o_dswa_s, *tail)

    return (y_prompt, y_sample, state_gla_prompt, cache_dswa_k_prompt, cache_dswa_v_prompt,
            state_gla_sample, tk, tv)
```

```python
import functools

import numpy as np
import jax
import jax.numpy as jnp
from jax import lax
from jax.experimental import pallas as pl
from jax.experimental.pallas import tpu as pltpu

F32 = jnp.float32
BF16 = jnp.bfloat16
I32 = jnp.int32

D_MODEL = 2048
GLA_HEADS = 4
GLA_DK = 128
GLA_DV = 256
GLA_WIDTH = GLA_HEADS * GLA_DV
GLA_QK = GLA_HEADS * GLA_DK
GLA_GATE_RANK = 16
GLA_TAU = 16.0
GLA_CHUNK = 64
DSWA_HEADS = 8
DSWA_HEAD_DIM = 128
DSWA_WIDTH = DSWA_HEADS * DSWA_HEAD_DIM
DSWA_BRANCHES = ((128, 1), (512, 4), (2048, 16))
N_EXPERTS = 32
TOP_K = 4
D_FF = D_MODEL
SWIGLU_LIMIT = 7.0
SWIGLU_ALPHA = 1.702
DEEPNORM_ALPHA = 2.0 ** 0.25
LN_EPS = 1e-5
RMS_EPS = 1e-6
NEG_BIG = -1e30

LANES = 128
V7X_VMEM_LIMIT = 56 * 1024 * 1024

PROJ_TM = 256
GLA_ROWS = 256
ATT_T = 256
OUT_TM = 256
MOE_TB = 256
MOE_TN = 1024
COMB_TM = 128


def _dot(a, b):
    return jnp.dot(a, b, preferred_element_type=F32)


def _dot_nt(a, b):
    return lax.dot_general(a, b, (((1,), (1,)), ((), ())), preferred_element_type=F32)


def _params(sem, vmem=V7X_VMEM_LIMIT):
    return pltpu.CompilerParams(dimension_semantics=sem, vmem_limit_bytes=vmem)


def _proj_gla_kernel(x_ref, wg_ref, wlr_ref, wa_ref, ba_ref, q_ref, k_ref, v_ref, r_ref, g_ref):
    xb = x_ref[...].astype(BF16)
    q_ref[...] = _dot(xb, wg_ref[:, 0:GLA_QK])
    k_ref[...] = _dot(xb, wg_ref[:, GLA_QK:2 * GLA_QK])
    v_ref[...] = _dot(xb, wg_ref[:, 2 * GLA_QK:2 * GLA_QK + GLA_WIDTH]).astype(v_ref.dtype)
    r_ref[...] = _dot(xb, wg_ref[:, 2 * GLA_QK + GLA_WIDTH:])
    lr = _dot(xb, wlr_ref[...])
    z = _dot(lr.astype(BF16), wa_ref[...]) + ba_ref[...]
    log_sig = jnp.minimum(z, 0.0) - jnp.log1p(jnp.exp(-jnp.abs(z)))
    g_ref[...] = log_sig * (1.0 / GLA_TAU)


def _proj_gla(x, wg, wlr, wa, ba, tm):
    m = x.shape[0]
    const = lambda i: (0, 0)
    row = lambda i: (i, 0)
    return pl.pallas_call(
        _proj_gla_kernel,
        grid=(m // tm,),
        in_specs=[pl.BlockSpec((tm, D_MODEL), row),
                  pl.BlockSpec(wg.shape, const),
                  pl.BlockSpec(wlr.shape, const),
                  pl.BlockSpec(wa.shape, const),
                  pl.BlockSpec(ba.shape, const)],
        out_specs=[pl.BlockSpec((tm, GLA_QK), row), pl.BlockSpec((tm, GLA_QK), row),
                   pl.BlockSpec((tm, GLA_WIDTH), row), pl.BlockSpec((tm, GLA_WIDTH), row),
                   pl.BlockSpec((tm, GLA_QK), row)],
        out_shape=[jax.ShapeDtypeStruct((m, GLA_QK), F32), jax.ShapeDtypeStruct((m, GLA_QK), F32),
                   jax.ShapeDtypeStruct((m, GLA_WIDTH), BF16), jax.ShapeDtypeStruct((m, GLA_WIDTH), F32),
                   jax.ShapeDtypeStruct((m, GLA_QK), F32)],
        compiler_params=_params(("parallel",)),
        name="proj_gla",
    )(x, wg, wlr, wa, ba)


def _proj_dswa_kernel(x_ref, w_ref, qb_ref, k_ref, v_ref, kb_ref, vb_ref):
    xb = x_ref[...].astype(BF16)
    w = DSWA_WIDTH
    qb_ref[...] = _dot(xb, w_ref[:, 0:w]).astype(BF16)
    k = _dot(xb, w_ref[:, w:2 * w])
    k_ref[...] = k
    kb_ref[...] = k.astype(BF16)
    v = _dot(xb, w_ref[:, 2 * w:3 * w])
    v_ref[...] = v
    vb_ref[...] = v.astype(BF16)


def _proj_dswa(x, w, tm):
    m = x.shape[0]
    row = lambda i: (i, 0)
    ospec = pl.BlockSpec((tm, DSWA_WIDTH), row)
    return pl.pallas_call(
        _proj_dswa_kernel,
        grid=(m // tm,),
        in_specs=[pl.BlockSpec((tm, D_MODEL), row), pl.BlockSpec(w.shape, lambda i: (0, 0))],
        out_specs=[ospec] * 5,
        out_shape=[jax.ShapeDtypeStruct((m, DSWA_WIDTH), BF16),
                   jax.ShapeDtypeStruct((m, DSWA_WIDTH), F32),
                   jax.ShapeDtypeStruct((m, DSWA_WIDTH), F32),
                   jax.ShapeDtypeStruct((m, DSWA_WIDTH), BF16),
                   jax.ShapeDtypeStruct((m, DSWA_WIDTH), BF16)],
        compiler_params=_params(("parallel",)),
        name="proj_dswa",
    )(x, w)


def _gla_constants():
    c = GLA_CHUNK
    t = np.arange(c)
    tril = (t[None, :] <= t[:, None]).astype(np.float32)
    rows = [t, np.full(c, 31), 32 * (t // 32) + 15, 16 * (t // 16) + 7, np.full(c, c - 1)]
    cum = np.concatenate([tril[r] for r in rows], axis=0)

    def level_mask(half):
        blk = 2 * half
        return ((t[:, None] // blk == t[None, :] // blk) & (t[:, None] % blk >= half)
                & (t[None, :] % blk < half)).astype(np.float32)

    lmask = np.stack([level_mask(32), level_mask(16), level_mask(8)])
    dmask = np.stack([((t[None, :] == 8 * (t[:, None] // 8) + s) & (t[:, None] % 8 >= s)).astype(np.float32)
                      for s in range(8)])
    return jnp.asarray(cum, BF16), jnp.asarray(lmask), jnp.asarray(dmask)


def _gla_kernel(q_ref, k_ref, g_ref, v_ref, r_ref, nw_ref, s0_ref, cum_ref, lmask_ref, dmask_ref,
                o_ref, sout_ref, s_sc, *, n_chunks, n_steps):
    step = pl.program_id(2)
    c = GLA_CHUNK

    @pl.when(step == 0)
    def _():
        s_sc[...] = s0_ref[0, 0]

    for ci in range(n_chunks):
        rows = slice(ci * c, (ci + 1) * c)
        q = q_ref[rows, :] * (GLA_DK ** -0.5)
        k = k_ref[rows, :]
        g = g_ref[rows, :]
        v = v_ref[rows, :]
        g1 = g.astype(BF16)
        e1 = g - g1.astype(F32)
        g2 = e1.astype(BF16)
        g3 = (e1 - g2.astype(F32)).astype(BF16)
        cb = _dot(cum_ref[...], jnp.concatenate([g1, g2, g3], axis=1))
        cb = cb[:, 0:GLA_DK] + cb[:, GLA_DK:2 * GLA_DK] + cb[:, 2 * GLA_DK:3 * GLA_DK]
        b = cb[0:c]
        b_end = cb[4 * c:5 * c]
        state = s_sc[...]
        o = _dot((q * jnp.exp(b)).astype(BF16), state.astype(BF16))
        att = jnp.zeros((c, c), F32)
        for li in range(3):
            bl = cb[(li + 1) * c:(li + 2) * c]
            ql = (q * jnp.exp(jnp.minimum(b - bl, 0.0))).astype(BF16)
            kl = (k * jnp.exp(jnp.minimum(bl - b, 0.0))).astype(BF16)
            att = att + _dot_nt(ql, kl) * lmask_ref[li]
        k3 = k.reshape(c // 8, 8, GLA_DK)
        b3 = b.reshape(c // 8, 8, GLA_DK)
        for sl in range(8):
            kb = jnp.broadcast_to(k3[:, sl:sl + 1, :], k3.shape).reshape(c, GLA_DK)
            bb = jnp.broadcast_to(b3[:, sl:sl + 1, :], b3.shape).reshape(c, GLA_DK)
            d = q * kb * jnp.exp(jnp.minimum(b - bb, 0.0))
            att = att + jnp.sum(d, axis=-1, keepdims=True) * dmask_ref[sl]
        o = o + _dot(att.astype(BF16), v)
        kd_t = (k * jnp.exp(b_end - b)).T.astype(BF16)
        e_col = jnp.exp(b_end).T[:, 0:1]
        s_sc[...] = state * e_col + _dot(kd_t, v)
        on = o * lax.rsqrt(jnp.mean(o * o, axis=-1, keepdims=True) + RMS_EPS) * nw_ref[...]
        r = r_ref[rows, :]
        o_ref[rows, :] = (on * (r * jax.nn.sigmoid(r))).astype(o_ref.dtype)

    @pl.when(step == n_steps - 1)
    def _():
        sout_ref[0, 0] = s_sc[...]


def _gla(q, k, g, v, r, nw, s0, batch, length, rows_per_step):
    n_steps = length // rows_per_step
    cum, lmask, dmask = _gla_constants()
    qk_spec = pl.BlockSpec((rows_per_step, GLA_DK), lambda b, h, t: (b * n_steps + t, h))
    v_spec = pl.BlockSpec((rows_per_step, GLA_DV), lambda b, h, t: (b * n_steps + t, h))
    s_spec = pl.BlockSpec((1, 1, GLA_DK, GLA_DV), lambda b, h, t: (b, h, 0, 0))
    const2 = lambda b, h, t: (0, 0)
    const3 = lambda b, h, t: (0, 0, 0)
    kern = functools.partial(_gla_kernel, n_chunks=rows_per_step // GLA_CHUNK, n_steps=n_steps)
    return pl.pallas_call(
        kern,
        grid=(batch, GLA_HEADS, n_steps),
        in_specs=[qk_spec, qk_spec, qk_spec, v_spec, v_spec,
                  pl.BlockSpec((1, GLA_DV), lambda b, h, t: (0, h)),
                  s_spec,
                  pl.BlockSpec(cum.shape, const2),
                  pl.BlockSpec(lmask.shape, const3),
                  pl.BlockSpec(dmask.shape, const3)],
        out_specs=[v_spec, s_spec],
        out_shape=[jax.ShapeDtypeStruct((batch * length, GLA_WIDTH), BF16),
                   jax.ShapeDtypeStruct((batch, GLA_HEADS, GLA_DK, GLA_DV), F32)],
        scratch_shapes=[pltpu.VMEM((GLA_DK, GLA_DV), F32)],
        compiler_params=_params(("parallel", "parallel", "arbitrary")),
        name="gla",
    )(q, k, g, v, r, nw, s0, cum, lmask, dmask)


def _alibi_slopes():
    return 2.0 ** (-8.0 * (np.arange(DSWA_HEADS, dtype=np.float64) + 1.0) / DSWA_HEADS)


def _branch_bias(dist):
    dist = np.asarray(dist, np.int64)
    mult = np.zeros(dist.shape, np.float64)
    for window, dil in DSWA_BRANCHES:
        mult += (dist >= 0) & (dist <= window) & (dist % dil == 0)
    slopes = _alibi_slopes().reshape((-1,) + (1,) * dist.ndim)
    bias = np.where(mult > 0, np.log(np.maximum(mult, 1.0)) - slopes * dist, NEG_BIG)
    return bias.astype(np.float32)


def _dswa_kernel(q_ref, k_ref, v_ref, bias_ref, o_ref, m_sc, l_sc, acc_sc, *, blk):
    qi = pl.program_id(2)
    m_sc[...] = jnp.full(m_sc.shape, NEG_BIG, F32)
    l_sc[...] = jnp.zeros(l_sc.shape, F32)
    acc_sc[...] = jnp.zeros(acc_sc.shape, F32)
    q = q_ref[...]
    scale = DSWA_HEAD_DIM ** -0.5

    def body(kj, carry):
        off = pl.multiple_of(kj * blk, blk)
        kb = k_ref[0, pl.ds(off, blk), :]
        vb = v_ref[0, pl.ds(off, blk), :]
        s = _dot_nt(q, kb) * scale + bias_ref[0, qi - kj]
        m_prev = m_sc[...]
        m_new = jnp.maximum(m_prev, jnp.max(s, axis=-1, keepdims=True))
        alpha = jnp.exp(m_prev - m_new)
        p = jnp.exp(s - m_new)
        l_sc[...] = alpha * l_sc[...] + jnp.sum(p, axis=-1, keepdims=True)
        acc_sc[...] = alpha * acc_sc[...] + _dot(p.astype(BF16), vb)
        m_sc[...] = m_new
        return carry

    lax.fori_loop(0, qi + 1, body, 0)
    o_ref[...] = (acc_sc[...] / l_sc[...]).astype(o_ref.dtype)


def _dswa_prompt(qb, kb, vb, batch, seq, blk):
    nq = seq // blk
    t = np.arange(blk)
    dist = (np.arange(nq)[:, None, None] * blk + t[None, :, None] - t[None, None, :])
    bias = jnp.asarray(_branch_bias(dist))
    k3 = kb.reshape(batch, seq, DSWA_WIDTH)
    v3 = vb.reshape(batch, seq, DSWA_WIDTH)
    q_spec = pl.BlockSpec((blk, DSWA_HEAD_DIM), lambda h, b, i: (b * nq + i, h))
    kv_spec = pl.BlockSpec((1, seq, DSWA_HEAD_DIM), lambda h, b, i: (b, 0, h))
    return pl.pallas_call(
        functools.partial(_dswa_kernel, blk=blk),
        grid=(DSWA_HEADS, batch, nq),
        in_specs=[q_spec, kv_spec, kv_spec,
                  pl.BlockSpec((1, nq, blk, blk), lambda h, b, i: (h, 0, 0, 0))],
        out_specs=q_spec,
        out_shape=jax.ShapeDtypeStruct((batch * seq, DSWA_WIDTH), BF16),
        scratch_shapes=[pltpu.VMEM((blk, 1), F32), pltpu.VMEM((blk, 1), F32),
                        pltpu.VMEM((blk, DSWA_HEAD_DIM), F32)],
        compiler_params=_params(("parallel", "parallel", "arbitrary")),
        name="dswa_prompt",
    )(qb, k3, v3, bias)


def _dswa_sample_kernel(qbd_ref, kc_ref, vc_ref, kn_ref, vn_ref, bc_ref, bn_ref, o_ref, *, n_new):
    scale = DSWA_HEAD_DIM ** -0.5
    qbd = qbd_ref[0]
    sc = _dot_nt(qbd, kc_ref[0].astype(BF16)) * scale + bc_ref[...]
    sn = _dot_nt(qbd, kn_ref[0]) * scale + bn_ref[...]
    m = jnp.maximum(jnp.max(sc, axis=-1, keepdims=True), jnp.max(sn, axis=-1, keepdims=True))
    pc = jnp.exp(sc - m)
    pn = jnp.exp(sn - m)
    den = jnp.sum(pc, axis=-1, keepdims=True) + jnp.sum(pn, axis=-1, keepdims=True)
    full = (_dot(pc.astype(BF16), vc_ref[0].astype(BF16)) + _dot(pn.astype(BF16), vn_ref[0])) / den
    head = lax.broadcasted_iota(I32, (DSWA_HEADS * n_new, 1), 0) // n_new
    out = jnp.zeros((DSWA_HEADS * n_new, DSWA_HEAD_DIM), F32)
    for h in range(DSWA_HEADS):
        out = out + jnp.where(head == h, full[:, h * DSWA_HEAD_DIM:(h + 1) * DSWA_HEAD_DIM], 0.0)
    o_ref[0] = out


def _dswa_sample(qb, k_new, v_new, cache_k, cache_v):
    batch, n_new, _ = qb.shape
    n_past = cache_k.shape[1]
    rows = DSWA_HEADS * n_new
    eye = jnp.eye(DSWA_HEADS, dtype=BF16)
    q5 = qb.reshape(batch, n_new, DSWA_HEADS, DSWA_HEAD_DIM).transpose(0, 2, 1, 3)
    qbd = (q5[:, :, :, None, :] * eye[None, :, None, :, None]).reshape(batch, rows, DSWA_WIDTH)
    pad = LANES - n_new
    kn = jnp.pad(k_new.astype(BF16), ((0, 0), (0, pad), (0, 0)))
    vn = jnp.pad(v_new.astype(BF16), ((0, 0), (0, pad), (0, 0)))
    lq = np.arange(n_new)
    bias_c = _branch_bias(n_past + lq[:, None] - np.arange(n_past)[None, :])
    bias_n = _branch_bias(lq[:, None] - np.arange(LANES)[None, :])
    bias_n[:, :, n_new:] = NEG_BIG
    bias_c = jnp.asarray(bias_c.reshape(rows, n_past))
    bias_n = jnp.asarray(bias_n.reshape(rows, LANES))
    b3 = lambda b: (b, 0, 0)
    c2 = lambda b: (0, 0)
    out = pl.pallas_call(
        functools.partial(_dswa_sample_kernel, n_new=n_new),
        grid=(batch,),
        in_specs=[pl.BlockSpec((1, rows, DSWA_WIDTH), b3),
                  pl.BlockSpec((1, n_past, DSWA_WIDTH), b3),
                  pl.BlockSpec((1, n_past, DSWA_WIDTH), b3),
                  pl.BlockSpec((1, LANES, DSWA_WIDTH), b3),
                  pl.BlockSpec((1, LANES, DSWA_WIDTH), b3),
                  pl.BlockSpec((rows, n_past), c2),
                  pl.BlockSpec((rows, LANES), c2)],
        out_specs=pl.BlockSpec((1, rows, DSWA_HEAD_DIM), b3),
        out_shape=jax.ShapeDtypeStruct((batch, rows, DSWA_HEAD_DIM), F32),
        compiler_params=_params(("parallel",)),
        name="dswa_sample",
    )(qbd, cache_k, cache_v, kn, vn, bias_c, bias_n)
    out = out.reshape(batch, DSWA_HEADS, n_new, DSWA_HEAD_DIM).transpose(0, 2, 1, 3)
    return out.reshape(batch, n_new, DSWA_WIDTH)


def _layer_norm(pre, g, b):
    mu = jnp.mean(pre, axis=-1, keepdims=True)
    cen = pre - mu
    var = jnp.mean(cen * cen, axis=-1, keepdims=True)
    return cen * lax.rsqrt(var + LN_EPS) * g + b


def _outproj_kernel(ogp_ref, odp_ref, xp_ref, ogs_ref, ods_ref, xs_ref, wo_ref, g_ref, b_ref,
                    wr_ref, br_ref, tri_ref, h_ref, route_ref, cnt_ref, cnt_sc, *, n_prompt_tiles):
    i = pl.program_id(0)
    is_s = i >= n_prompt_tiles

    @pl.when(i == 0)
    def _():
        cnt_sc[...] = jnp.zeros(cnt_sc.shape, F32)

    og = jnp.where(is_s, ogs_ref[...], ogp_ref[...])
    od = jnp.where(is_s, ods_ref[...], odp_ref[...])
    x = jnp.where(is_s, xs_ref[...], xp_ref[...])
    acc = _dot(og, wo_ref[0:GLA_WIDTH, :]) + _dot(od, wo_ref[GLA_WIDTH:, :])
    h = _layer_norm(DEEPNORM_ALPHA * x + acc, g_ref[...], b_ref[...])
    h_ref[...] = h
    logits = _dot(h.astype(BF16), wr_ref[...]) + br_ref[...]
    lane = lax.broadcasted_iota(I32, logits.shape, 1).astype(F32)
    tops, sels, idxs = [], [], []
    cur = logits
    for _ in range(TOP_K):
        mx = jnp.max(cur, axis=-1, keepdims=True)
        idx = jnp.min(jnp.where(cur == mx, lane, float(LANES)), axis=-1, keepdims=True)
        sel = lane == idx
        tops.append(mx)
        sels.append(sel)
        idxs.append(idx)
        cur = jnp.where(sel, -jnp.inf, cur)
    exps = [jnp.exp(t - tops[0]) for t in tops]
    den = exps[0] + exps[1] + exps[2] + exps[3]
    multi = jnp.zeros(logits.shape, F32)
    for sel in sels:
        multi = multi + jnp.where(sel, 1.0, 0.0)
    rank_all = _dot(tri_ref[...], multi.astype(BF16)) + cnt_sc[...]
    cnt_sc[...] = cnt_sc[...] + jnp.sum(multi, axis=0, keepdims=True)
    cnt_ref[...] = cnt_sc[...]
    route = jnp.zeros(logits.shape, F32)
    for kk in range(TOP_K):
        rank = jnp.sum(jnp.where(sels[kk], rank_all, 0.0), axis=-1, keepdims=True)
        route = jnp.where(lane == kk, idxs[kk], route)
        route = jnp.where(lane == TOP_K + kk, exps[kk] / den, route)
        route = jnp.where(lane == 2 * TOP_K + kk, rank, route)
    route_ref[...] = route


def _outproj(og_p, od_p, x_p, og_s, od_s, x_s, wo, g, b, wr, br, tm):
    n_p = x_p.shape[0] // tm
    tri = jnp.asarray(np.tril(np.ones((tm, tm), np.float32), -1), BF16)
    prow = lambda i: (jnp.minimum(i, n_p - 1), 0)
    c2 = lambda i: (0, 0)
    row = lambda i: (i, 0)
    m_tot = x_p.shape[0] + tm
    return pl.pallas_call(
        functools.partial(_outproj_kernel, n_prompt_tiles=n_p),
        grid=(n_p + 1,),
        in_specs=[pl.BlockSpec((tm, GLA_WIDTH), prow), pl.BlockSpec((tm, DSWA_WIDTH), prow),
                  pl.BlockSpec((tm, D_MODEL), prow),
                  pl.BlockSpec((tm, GLA_WIDTH), c2), pl.BlockSpec((tm, DSWA_WIDTH), c2),
                  pl.BlockSpec((tm, D_MODEL), c2),
                  pl.BlockSpec(wo.shape, c2), pl.BlockSpec(g.shape, c2), pl.BlockSpec(b.shape, c2),
                  pl.BlockSpec(wr.shape, c2), pl.BlockSpec(br.shape, c2), pl.BlockSpec(tri.shape, c2)],
        out_specs=[pl.BlockSpec((tm, D_MODEL), row), pl.BlockSpec((tm, LANES), row),
                   pl.BlockSpec((1, LANES), c2)],
        out_shape=[jax.ShapeDtypeStruct((m_tot, D_MODEL), F32),
                   jax.ShapeDtypeStruct((m_tot, LANES), F32),
                   jax.ShapeDtypeStruct((1, LANES), F32)],
        scratch_shapes=[pltpu.VMEM((1, LANES), F32)],
        compiler_params=_params(("arbitrary",)),
        name="outproj_ln_router",
    )(og_p, od_p, x_p, og_s, od_s, x_s, wo, g, b, wr, br, tri)


def _dispatch_kernel(cnt_ref, pst_ref, dest_ref, h_ref, xr_ref, zero_sc, sem, zsem, *, tm, tb, n_blocks):
    i = pl.program_id(0)

    def row_copy(t, r):
        return pltpu.make_async_copy(h_ref.at[pl.ds(t, 1), :], xr_ref.at[pl.ds(r, 1), :], sem)

    def issue(t, carry):
        for kk in range(TOP_K):
            row_copy(t, dest_ref[0, 0, t * TOP_K + kk]).start()
        return carry

    lax.fori_loop(0, tm, issue, 0)

    def zero_copy(r):
        return pltpu.make_async_copy(zero_sc.at[pl.ds(0, 1), :], xr_ref.at[pl.ds(r, 1), :], zsem)

    @pl.when(i == 0)
    def _():
        zero_sc[...] = jnp.zeros(zero_sc.shape, F32)

        def per_expert(e, carry):
            cnt = cnt_ref[e]
            base = pst_ref[e]
            end = (cnt + tb - 1) // tb * tb

            def start(r, c2):
                zero_copy(base + r).start()
                return c2

            def wait(r, c2):
                zero_copy(base + r).wait()
                return c2

            lax.fori_loop(cnt, end, start, 0)
            lax.fori_loop(cnt, end, wait, 0)
            return carry

        lax.fori_loop(0, N_EXPERTS, per_expert, 0)

        last = N_EXPERTS - 1
        n_used = (pst_ref[last] + (cnt_ref[last] + tb - 1) // tb * tb) // tb

        def tail_copy(blk):
            return pltpu.make_async_copy(zero_sc, xr_ref.at[pl.ds(pl.multiple_of(blk * tb, tb), tb), :], zsem)

        def tail_start(blk, c2):
            tail_copy(blk).start()
            return c2

        def tail_wait(blk, c2):
            tail_copy(blk).wait()
            return c2

        lax.fori_loop(n_used, n_blocks, tail_start, 0)
        lax.fori_loop(n_used, n_blocks, tail_wait, 0)

    def drain(t, carry):
        for kk in range(TOP_K):
            row_copy(t, 0).wait()
        return carry

    lax.fori_loop(0, tm, drain, 0)


def _dispatch(h, dest, counts, pstart, n_rows, tm, tb):
    m = h.shape[0]
    n_t = m // tm
    dest3 = dest.reshape(n_t, 1, tm * TOP_K)
    grid_spec = pltpu.PrefetchScalarGridSpec(
        num_scalar_prefetch=2,
        grid=(n_t,),
        in_specs=[pl.BlockSpec((1, 1, tm * TOP_K), lambda i, c, p: (i, 0, 0), memory_space=pltpu.SMEM),
                  pl.BlockSpec((tm, D_MODEL), lambda i, c, p: (i, 0))],
        out_specs=pl.BlockSpec(memory_space=pl.ANY),
        scratch_shapes=[pltpu.VMEM((tb, D_MODEL), F32), pltpu.SemaphoreType.DMA(()),
                        pltpu.SemaphoreType.DMA(())],
    )
    return pl.pallas_call(
        functools.partial(_dispatch_kernel, tm=tm, tb=tb, n_blocks=n_rows // tb),
        grid_spec=grid_spec,
        out_shape=jax.ShapeDtypeStruct((n_rows, D_MODEL), F32),
        compiler_params=_params(("arbitrary",)),
        name="moe_dispatch",
    )(counts, pstart, dest3, h)


def _expert_up_kernel(be_ref, nvb_ref, x_ref, wg_ref, wu_ref, bg_ref, bu_ref, a_ref, wgb_sc, wub_sc):
    i = pl.program_id(1)
    prev = be_ref[jnp.maximum(i - 1, 0)]

    @pl.when((i == 0) | (be_ref[i] != prev))
    def _():
        wgb_sc[...] = wg_ref[0].astype(BF16)
        wub_sc[...] = wu_ref[0].astype(BF16)

    @pl.when(i < nvb_ref[0])
    def _():
        xb = x_ref[...].astype(BF16)
        gate = jnp.minimum(_dot(xb, wgb_sc[...]) + bg_ref[0], SWIGLU_LIMIT)
        up = jnp.clip(_dot(xb, wub_sc[...]) + bu_ref[0], -SWIGLU_LIMIT, SWIGLU_LIMIT)
        a_ref[...] = (gate * jax.nn.sigmoid(SWIGLU_ALPHA * gate) * (up + 1.0)).astype(a_ref.dtype)

    @pl.when(i >= nvb_ref[0])
    def _():
        a_ref[...] = jnp.zeros(a_ref.shape, a_ref.dtype)


def _expert_up(x_rows, blk_e, nvb, wg, wu, bg, bu, tb, tn):
    n_rows = x_rows.shape[0]
    nb = n_rows // tb
    nj = D_FF // tn
    rowblk = lambda j, i, be, nv: (jnp.minimum(i, nv[0] - 1), 0)
    wspec = pl.BlockSpec((1, D_MODEL, tn), lambda j, i, be, nv: (be[i], 0, j))
    bspec = pl.BlockSpec((1, 1, tn), lambda j, i, be, nv: (be[i], 0, j))
    grid_spec = pltpu.PrefetchScalarGridSpec(
        num_scalar_prefetch=2,
        grid=(nj, nb),
        in_specs=[pl.BlockSpec((tb, D_MODEL), rowblk), wspec, wspec, bspec, bspec],
        out_specs=pl.BlockSpec((tb, tn), lambda j, i, be, nv: (i, j)),
        scratch_shapes=[pltpu.VMEM((D_MODEL, tn), BF16), pltpu.VMEM((D_MODEL, tn), BF16)],
    )
    return pl.pallas_call(
        _expert_up_kernel,
        grid_spec=grid_spec,
        out_shape=jax.ShapeDtypeStruct((n_rows, D_FF), BF16),
        compiler_params=_params(("arbitrary", "arbitrary")),
        name="moe_up",
    )(blk_e, nvb, x_rows, wg, wu, bg.reshape(N_EXPERTS, 1, D_FF), bu.reshape(N_EXPERTS, 1, D_FF))


def _expert_down_kernel(be_ref, nvb_ref, a_ref, wd_ref, bd_ref, y_ref, wdb_sc):
    i = pl.program_id(1)
    prev = be_ref[jnp.maximum(i - 1, 0)]

    @pl.when((i == 0) | (be_ref[i] != prev))
    def _():
        wdb_sc[...] = wd_ref[0].astype(BF16)

    @pl.when(i < nvb_ref[0])
    def _():
        y_ref[...] = _dot(a_ref[...], wdb_sc[...]) + bd_ref[0]

    @pl.when(i >= nvb_ref[0])
    def _():
        y_ref[...] = jnp.zeros(y_ref.shape, y_ref.dtype)


def _expert_down(a_rows, blk_e, nvb, wd, bd, tb, tn):
    n_rows = a_rows.shape[0]
    nb = n_rows // tb
    nj = D_MODEL // tn
    grid_spec = pltpu.PrefetchScalarGridSpec(
        num_scalar_prefetch=2,
        grid=(nj, nb),
        in_specs=[pl.BlockSpec((tb, D_FF), lambda j, i, be, nv: (jnp.minimum(i, nv[0] - 1), 0)),
                  pl.BlockSpec((1, D_FF, tn), lambda j, i, be, nv: (be[i], 0, j)),
                  pl.BlockSpec((1, 1, tn), lambda j, i, be, nv: (be[i], 0, j))],
        out_specs=pl.BlockSpec((tb, tn), lambda j, i, be, nv: (i, j)),
        scratch_shapes=[pltpu.VMEM((D_FF, tn), BF16)],
    )
    return pl.pallas_call(
        _expert_down_kernel,
        grid_spec=grid_spec,
        out_shape=jax.ShapeDtypeStruct((n_rows, D_MODEL), F32),
        compiler_params=_params(("arbitrary", "arbitrary")),
        name="moe_down",
    )(blk_e, nvb, a_rows, wd, bd.reshape(N_EXPERTS, 1, D_MODEL))


def _combine_kernel(dcur_ref, dnxt_ref, y_ref, h_ref, gate_ref, g_ref, b_ref, yp_ref, ys_ref,
                    buf, sems, *, tm, n_tiles, n_prompt_tiles):
    i = pl.program_id(0)
    slot = i % 2

    def row_copy(dref, t, kk, sl):
        r = dref[0, 0, t * TOP_K + kk]
        return pltpu.make_async_copy(y_ref.at[pl.ds(r, 1), :], buf.at[sl, kk, pl.ds(t, 1), :], sems.at[sl])

    def issue(dref, sl):
        def body(t, carry):
            for kk in range(TOP_K):
                row_copy(dref, t, kk, sl).start()
            return carry
        lax.fori_loop(0, tm, body, 0)

    @pl.when(i == 0)
    def _():
        issue(dcur_ref, 0)

    @pl.when(i + 1 < n_tiles)
    def _():
        issue(dnxt_ref, 1 - slot)

    def drain(t, carry):
        for kk in range(TOP_K):
            row_copy(dcur_ref, t, kk, slot).wait()
        return carry

    lax.fori_loop(0, tm, drain, 0)
    gate = gate_ref[...]
    f = jnp.zeros((tm, D_MODEL), F32)
    for kk in range(TOP_K):
        f = f + buf[slot, kk] * gate[:, TOP_K + kk:TOP_K + kk + 1]
    y = _layer_norm(DEEPNORM_ALPHA * h_ref[...] + f, g_ref[...], b_ref[...])

    @pl.when(i < n_prompt_tiles)
    def _():
        yp_ref[...] = y

    @pl.when(i >= n_prompt_tiles)
    def _():
        ys_ref[...] = y


def _combine(y_rows, h, route, dest, g, b, n_prompt, n_sample, tm):
    n_p = n_prompt // tm
    n_tiles = n_p + n_sample // tm
    dest3 = dest.reshape(-1, 1, tm * TOP_K)
    smem = functools.partial(pl.BlockSpec, (1, 1, tm * TOP_K), memory_space=pltpu.SMEM)
    c2 = lambda i: (0, 0)
    return pl.pallas_call(
        functools.partial(_combine_kernel, tm=tm, n_tiles=n_tiles, n_prompt_tiles=n_p),
        grid=(n_tiles,),
        in_specs=[smem(lambda i: (i, 0, 0)),
                  smem(lambda i: (jnp.minimum(i + 1, n_tiles - 1), 0, 0)),
                  pl.BlockSpec(memory_space=pl.ANY),
                  pl.BlockSpec((tm, D_MODEL), lambda i: (i, 0)),
                  pl.BlockSpec((tm, LANES), lambda i: (i, 0)),
                  pl.BlockSpec(g.shape, c2), pl.BlockSpec(b.shape, c2)],
        out_specs=[pl.BlockSpec((tm, D_MODEL), lambda i: (jnp.minimum(i, n_p - 1), 0)),
                   pl.BlockSpec((tm, D_MODEL), lambda i: (jnp.maximum(i - n_p, 0), 0))],
        out_shape=[jax.ShapeDtypeStruct((n_prompt, D_MODEL), F32),
                   jax.ShapeDtypeStruct((n_sample, D_MODEL), F32)],
        scratch_shapes=[pltpu.VMEM((2, TOP_K, tm, D_MODEL), F32), pltpu.SemaphoreType.DMA((2,))],
        compiler_params=_params(("arbitrary",)),
        name="moe_combine_ln",
    )(dest3, dest3, y_rows, h, route, g, b)


def _moe_and_norm(h, route, cnt, n_prompt, n_sample, w_exp_gate, b_exp_gate, w_exp_up, b_exp_up,
                  w_exp_down, b_exp_down, ln2_g, ln2_b, tb, tn, out_tm, comb_tm):
    m_tot = h.shape[0]
    top_e = route[:, 0:TOP_K].astype(I32)
    rank = route[:, 2 * TOP_K:3 * TOP_K].astype(I32)
    counts = cnt[0, 0:N_EXPERTS].astype(I32)
    padded = (counts + tb - 1) // tb * tb
    pend = jnp.cumsum(padded)
    pstart = pend - padded
    dest = (pstart[top_e] + rank).reshape(-1)
    nb = (m_tot * TOP_K + N_EXPERTS * (tb - 1) + tb - 1) // tb
    nvb = (pend[-1:] // tb).astype(I32)
    blk_e = jnp.minimum(jnp.searchsorted(pend, jnp.arange(nb, dtype=I32) * tb, side='right'),
                        N_EXPERTS - 1).astype(I32)
    x_rows = _dispatch(h, dest, counts, pstart.astype(I32), nb * tb, out_tm, tb)
    a_rows = _expert_up(x_rows, blk_e, nvb, w_exp_gate, w_exp_up, b_exp_gate, b_exp_up, tb, tn)
    y_rows = _expert_down(a_rows, blk_e, nvb, w_exp_down, b_exp_down, tb, tn)
    return _combine(y_rows, h, route, dest, ln2_g, ln2_b, n_prompt, n_sample, comb_tm)


def _split_w_in(w_in):
    sizes = (GLA_QK, GLA_QK, GLA_WIDTH, GLA_GATE_RANK, GLA_WIDTH, DSWA_WIDTH, DSWA_WIDTH, DSWA_WIDTH)
    o = np.cumsum((0,) + sizes)
    wg = jnp.concatenate([w_in[:, o[0]:o[3]], w_in[:, o[4]:o[5]]], axis=1).astype(BF16)
    wlr = jnp.pad(w_in[:, o[3]:o[4]], ((0, 0), (0, LANES - GLA_GATE_RANK))).astype(BF16)
    wd = w_in[:, o[5]:o[8]].astype(BF16)
    return wg, wlr, wd


def kernel(x_prompt, x_sample, state_gla, cache_dswa_k, cache_dswa_v, w_in, w_gla_alpha, b_gla_alpha,
           gla_norm_w, w_o, ln1_g, ln1_b, w_router, b_router, w_exp_gate, b_exp_gate, w_exp_up,
           b_exp_up, w_exp_down, b_exp_down, ln2_g, ln2_b):
    bp, sp, _ = x_prompt.shape
    bs, ls, _ = x_sample.shape
    n_prompt, n_sample = bp * sp, bs * ls
    n_past = cache_dswa_k.shape[1]

    wg, wlr, wd = _split_w_in(w_in)
    wa = jnp.pad(w_gla_alpha, ((0, LANES - GLA_GATE_RANK), (0, 0))).astype(BF16)
    ba = b_gla_alpha.reshape(1, GLA_QK)
    nw = gla_norm_w.reshape(1, GLA_WIDTH)
    xp = x_prompt.reshape(n_prompt, D_MODEL)
    xs = x_sample.reshape(n_sample, D_MODEL)

    q, k, v, r, g = _proj_gla(xp, wg, wlr, wa, ba, PROJ_TM)
    og_p, state_p = _gla(q, k, g, v, r, nw, jnp.zeros((bp, GLA_HEADS, GLA_DK, GLA_DV), F32),
                         bp, sp, GLA_ROWS)
    dqb, dk, dv, dkb, dvb = _proj_dswa(xp, wd, PROJ_TM)
    od_p = _dswa_prompt(dqb, dkb, dvb, bp, sp, ATT_T)

    qs, ks, vs, rs, gs = _proj_gla(xs, wg, wlr, wa, ba, n_sample)
    lpad = GLA_CHUNK - ls
    pad_rows = lambda a: jnp.pad(a.reshape(bs, ls, -1), ((0, 0), (0, lpad), (0, 0))).reshape(bs * GLA_CHUNK, -1)
    og_s_pad, state_s = _gla(pad_rows(qs), pad_rows(ks), pad_rows(gs), pad_rows(vs), pad_rows(rs), nw,
                             state_gla, bs, GLA_CHUNK, GLA_CHUNK)
    og_s = og_s_pad.reshape(bs, GLA_CHUNK, GLA_WIDTH)[:, :ls].reshape(n_sample, GLA_WIDTH)
    tqb, tk, tv, _, _ = _proj_dswa(xs, wd, n_sample)
    od_s = _dswa_sample(tqb.reshape(bs, ls, DSWA_WIDTH), tk.reshape(bs, ls, DSWA_WIDTH),
                        tv.reshape(bs, ls, DSWA_WIDTH),
                        cache_dswa_k.reshape(bs, n_past, DSWA_WIDTH),
                        cache_dswa_v.reshape(bs, n_past, DSWA_WIDTH))
    od_s = od_s.reshape(n_sample, DSWA_WIDTH).astype(BF16)

    spad = ((0, OUT_TM - n_sample), (0, 0))
    wr = jnp.pad(w_router, ((0, 0), (0, LANES - N_EXPERTS))).astype(BF16)
    br = jnp.pad(b_router, (0, LANES - N_EXPERTS), constant_values=NEG_BIG).reshape(1, LANES)
    h, route, cnt = _outproj(og_p, od_p, xp, jnp.pad(og_s, spad), jnp.pad(od_s, spad), jnp.pad(xs, spad),
                             w_o.astype(BF16), ln1_g.reshape(1, D_MODEL), ln1_b.reshape(1, D_MODEL),
                             wr, br, OUT_TM)
    y_p, y_s = _moe_and_norm(h, route, cnt, n_prompt, n_sample, w_exp_gate, b_exp_gate, w_exp_up,
                             b_exp_up, w_exp_down, b_exp_down, ln2_g.reshape(1, D_MODEL),
                             ln2_b.reshape(1, D_MODEL), MOE_TB, MOE_TN, OUT_TM, COMB_TM)

    return (y_p.reshape(bp, sp, D_MODEL), y_s.reshape(bs, ls, D_MODEL), state_p,
            dk.reshape(bp, sp, DSWA_HEADS, DSWA_HEAD_DIM), dv.reshape(bp, sp, DSWA_HEADS, DSWA_HEAD_DIM),
            state_s, tk.reshape(bs, ls, DSWA_HEADS, DSWA_HEAD_DIM),
            tv.reshape(bs, ls, DSWA_HEADS, DSWA_HEAD_DIM))
```

```python
import functools

import numpy as np
import jax
import jax.numpy as jnp
from jax import lax
from jax.experimental import pallas as pl
from jax.experimental.pallas import tpu as pltpu

F32 = jnp.float32
BF16 = jnp.bfloat16
I32 = jnp.int32

D_MODEL = 2048
GLA_HEADS = 4
GLA_DK = 128
GLA_DV = 256
GLA_WIDTH = GLA_HEADS * GLA_DV
GLA_QK = GLA_HEADS * GLA_DK
GLA_GATE_RANK = 16
GLA_TAU = 16.0
GLA_CHUNK = 64
DSWA_HEADS = 8
DSWA_HEAD_DIM = 128
DSWA_WIDTH = DSWA_HEADS * DSWA_HEAD_DIM
DSWA_BRANCHES = ((128, 1), (512, 4), (2048, 16))
N_EXPERTS = 32
TOP_K = 4
D_FF = D_MODEL
SWIGLU_LIMIT = 7.0
SWIGLU_ALPHA = 1.702
DEEPNORM_ALPHA = 2.0 ** 0.25
LN_EPS = 1e-5
RMS_EPS = 1e-6
NEG_BIG = -1e30

LANES = 128
V7X_VMEM_LIMIT = 56 * 1024 * 1024

PROJ_TM = 256
GLA_ROWS = 256
ATT_T = 256
ATT_HEADS = 4
OUT_TM = 256
MOE_TB = 256
MOE_TN = 1024
COMB_TM = 128


def _dot(a, b):
    return jnp.dot(a, b, preferred_element_type=F32)


def _dot_nt(a, b):
    return lax.dot_general(a, b, (((1,), (1,)), ((), ())), preferred_element_type=F32)


def _params(sem, vmem=V7X_VMEM_LIMIT):
    return pltpu.CompilerParams(dimension_semantics=sem, vmem_limit_bytes=vmem)


def _proj_gla_kernel(x_ref, wg_ref, wlr_ref, wa_ref, ba_ref, q_ref, k_ref, v_ref, r_ref, g_ref):
    xb = x_ref[...].astype(BF16)
    q_ref[...] = _dot(xb, wg_ref[:, 0:GLA_QK])
    k_ref[...] = _dot(xb, wg_ref[:, GLA_QK:2 * GLA_QK])
    v_ref[...] = _dot(xb, wg_ref[:, 2 * GLA_QK:2 * GLA_QK + GLA_WIDTH]).astype(v_ref.dtype)
    r_ref[...] = _dot(xb, wg_ref[:, 2 * GLA_QK + GLA_WIDTH:])
    lr = _dot(xb, wlr_ref[...])
    z = _dot(lr.astype(BF16), wa_ref[...]) + ba_ref[...]
    log_sig = jnp.minimum(z, 0.0) - jnp.log1p(jnp.exp(-jnp.abs(z)))
    g_ref[...] = log_sig * (1.0 / GLA_TAU)


def _proj_gla(x, wg, wlr, wa, ba, tm):
    m = x.shape[0]
    const = lambda i: (0, 0)
    row = lambda i: (i, 0)
    return pl.pallas_call(
        _proj_gla_kernel,
        grid=(m // tm,),
        in_specs=[pl.BlockSpec((tm, D_MODEL), row),
                  pl.BlockSpec(wg.shape, const),
                  pl.BlockSpec(wlr.shape, const),
                  pl.BlockSpec(wa.shape, const),
                  pl.BlockSpec(ba.shape, const)],
        out_specs=[pl.BlockSpec((tm, GLA_QK), row), pl.BlockSpec((tm, GLA_QK), row),
                   pl.BlockSpec((tm, GLA_WIDTH), row), pl.BlockSpec((tm, GLA_WIDTH), row),
                   pl.BlockSpec((tm, GLA_QK), row)],
        out_shape=[jax.ShapeDtypeStruct((m, GLA_QK), F32), jax.ShapeDtypeStruct((m, GLA_QK), F32),
                   jax.ShapeDtypeStruct((m, GLA_WIDTH), BF16), jax.ShapeDtypeStruct((m, GLA_WIDTH), F32),
                   jax.ShapeDtypeStruct((m, GLA_QK), F32)],
        compiler_params=_params(("parallel",)),
        name="proj_gla",
    )(x, wg, wlr, wa, ba)


def _proj_dswa_kernel(x_ref, w_ref, qb_ref, k_ref, v_ref, kb_ref, vb_ref):
    xb = x_ref[...].astype(BF16)
    w = DSWA_WIDTH
    qb_ref[...] = (_dot(xb, w_ref[:, 0:w]) * (DSWA_HEAD_DIM ** -0.5)).astype(BF16)
    k = _dot(xb, w_ref[:, w:2 * w])
    k_ref[...] = k
    kb_ref[...] = k.astype(BF16)
    v = _dot(xb, w_ref[:, 2 * w:3 * w])
    v_ref[...] = v
    vb_ref[...] = v.astype(BF16)


def _proj_dswa(x, w, tm):
    m = x.shape[0]
    row = lambda i: (i, 0)
    ospec = pl.BlockSpec((tm, DSWA_WIDTH), row)
    return pl.pallas_call(
        _proj_dswa_kernel,
        grid=(m // tm,),
        in_specs=[pl.BlockSpec((tm, D_MODEL), row), pl.BlockSpec(w.shape, lambda i: (0, 0))],
        out_specs=[ospec] * 5,
        out_shape=[jax.ShapeDtypeStruct((m, DSWA_WIDTH), BF16),
                   jax.ShapeDtypeStruct((m, DSWA_WIDTH), F32),
                   jax.ShapeDtypeStruct((m, DSWA_WIDTH), F32),
                   jax.ShapeDtypeStruct((m, DSWA_WIDTH), BF16),
                   jax.ShapeDtypeStruct((m, DSWA_WIDTH), BF16)],
        compiler_params=_params(("parallel",)),
        name="proj_dswa",
    )(x, w)


def _gla_constants():
    c = GLA_CHUNK
    t = np.arange(c)
    tril = (t[None, :] <= t[:, None]).astype(np.float32)
    rows = [t, np.full(c, 31), 32 * (t // 32) + 15, 16 * (t // 16) + 7, np.full(c, c - 1)]
    cum = np.concatenate([tril[r] for r in rows], axis=0)

    def level_mask(half):
        blk = 2 * half
        return ((t[:, None] // blk == t[None, :] // blk) & (t[:, None] % blk >= half)
                & (t[None, :] % blk < half)).astype(np.float32)

    lmask = np.stack([level_mask(32), level_mask(16), level_mask(8)])
    dmask = np.stack([((t[None, :] == 8 * (t[:, None] // 8) + s) & (t[:, None] % 8 >= s)).astype(np.float32)
                      for s in range(8)])
    return jnp.asarray(cum, BF16), jnp.asarray(lmask), jnp.asarray(dmask)


def _gla_kernel(q_ref, k_ref, g_ref, v_ref, r_ref, nw_ref, s0_ref, cum_ref, lmask_ref, dmask_ref,
                o_ref, sout_ref, s_sc, *, n_chunks, n_steps):
    step = pl.program_id(2)
    c = GLA_CHUNK

    @pl.when(step == 0)
    def _():
        s_sc[...] = s0_ref[0, 0]

    for ci in range(n_chunks):
        rows = slice(ci * c, (ci + 1) * c)
        q = q_ref[rows, :] * (GLA_DK ** -0.5)
        k = k_ref[rows, :]
        g = g_ref[rows, :]
        v = v_ref[rows, :]
        g1 = g.astype(BF16)
        e1 = g - g1.astype(F32)
        g2 = e1.astype(BF16)
        g3 = (e1 - g2.astype(F32)).astype(BF16)
        cb = _dot(cum_ref[...], jnp.concatenate([g1, g2, g3], axis=1))
        cb = cb[:, 0:GLA_DK] + cb[:, GLA_DK:2 * GLA_DK] + cb[:, 2 * GLA_DK:3 * GLA_DK]
        b = cb[0:c]
        b_end = cb[4 * c:5 * c]
        state = s_sc[...]
        o = _dot((q * jnp.exp(b)).astype(BF16), state.astype(BF16))
        att = jnp.zeros((c, c), F32)
        for li in range(3):
            bl = cb[(li + 1) * c:(li + 2) * c]
            ql = (q * jnp.exp(jnp.minimum(b - bl, 0.0))).astype(BF16)
            kl = (k * jnp.exp(jnp.minimum(bl - b, 0.0))).astype(BF16)
            att = att + _dot_nt(ql, kl) * lmask_ref[li]
        k3 = k.reshape(c // 8, 8, GLA_DK)
        b3 = b.reshape(c // 8, 8, GLA_DK)
        for sl in range(8):
            kb = jnp.broadcast_to(k3[:, sl:sl + 1, :], k3.shape).reshape(c, GLA_DK)
            bb = jnp.broadcast_to(b3[:, sl:sl + 1, :], b3.shape).reshape(c, GLA_DK)
            d = q * kb * jnp.exp(jnp.minimum(b - bb, 0.0))
            att = att + jnp.sum(d, axis=-1, keepdims=True) * dmask_ref[sl]
        o = o + _dot(att.astype(BF16), v)
        kd_t = (k * jnp.exp(b_end - b)).T.astype(BF16)
        e_col = jnp.exp(b_end).T[:, 0:1]
        s_sc[...] = state * e_col + _dot(kd_t, v)
        on = o * lax.rsqrt(jnp.mean(o * o, axis=-1, keepdims=True) + RMS_EPS) * nw_ref[...]
        r = r_ref[rows, :]
        o_ref[rows, :] = (on * (r * jax.nn.sigmoid(r))).astype(o_ref.dtype)

    @pl.when(step == n_steps - 1)
    def _():
        sout_ref[0, 0] = s_sc[...]


def _gla(q, k, g, v, r, nw, s0, batch, length, rows_per_step):
    n_steps = length // rows_per_step
    cum, lmask, dmask = _gla_constants()
    qk_spec = pl.BlockSpec((rows_per_step, GLA_DK), lambda b, h, t: (b * n_steps + t, h))
    v_spec = pl.BlockSpec((rows_per_step, GLA_DV), lambda b, h, t: (b * n_steps + t, h))
    s_spec = pl.BlockSpec((1, 1, GLA_DK, GLA_DV), lambda b, h, t: (b, h, 0, 0))
    const2 = lambda b, h, t: (0, 0)
    const3 = lambda b, h, t: (0, 0, 0)
    kern = functools.partial(_gla_kernel, n_chunks=rows_per_step // GLA_CHUNK, n_steps=n_steps)
    return pl.pallas_call(
        kern,
        grid=(batch, GLA_HEADS, n_steps),
        in_specs=[qk_spec, qk_spec, qk_spec, v_spec, v_spec,
                  pl.BlockSpec((1, GLA_DV), lambda b, h, t: (0, h)),
                  s_spec,
                  pl.BlockSpec(cum.shape, const2),
                  pl.BlockSpec(lmask.shape, const3),
                  pl.BlockSpec(dmask.shape, const3)],
        out_specs=[v_spec, s_spec],
        out_shape=[jax.ShapeDtypeStruct((batch * length, GLA_WIDTH), BF16),
                   jax.ShapeDtypeStruct((batch, GLA_HEADS, GLA_DK, GLA_DV), F32)],
        scratch_shapes=[pltpu.VMEM((GLA_DK, GLA_DV), F32)],
        compiler_params=_params(("parallel", "parallel", "arbitrary")),
        name="gla",
    )(q, k, g, v, r, nw, s0, cum, lmask, dmask)


def _alibi_slopes():
    return 2.0 ** (-8.0 * (np.arange(DSWA_HEADS, dtype=np.float64) + 1.0) / DSWA_HEADS)


def _branch_bias(dist):
    dist = np.asarray(dist, np.int64)
    mult = np.zeros(dist.shape, np.float64)
    for window, dil in DSWA_BRANCHES:
        mult += (dist >= 0) & (dist <= window) & (dist % dil == 0)
    slopes = _alibi_slopes().reshape((-1,) + (1,) * dist.ndim)
    bias = np.where(mult > 0, np.log(np.maximum(mult, 1.0)) - slopes * dist, NEG_BIG)
    return bias.astype(np.float32)


def _dswa_kernel(q_ref, k_ref, v_ref, bias_ref, o_ref, m_sc, l_sc, acc_sc, *, blk, heads):
    qi = pl.program_id(2)
    e = DSWA_HEAD_DIM
    m_sc[...] = jnp.full(m_sc.shape, NEG_BIG, F32)
    l_sc[...] = jnp.zeros(l_sc.shape, F32)
    acc_sc[...] = jnp.zeros(acc_sc.shape, F32)

    def body(kj, carry):
        off = pl.multiple_of(kj * blk, blk)
        for g in range(heads):
            cols = slice(g * e, (g + 1) * e)
            kb = k_ref[0, pl.ds(off, blk), cols]
            vb = v_ref[0, pl.ds(off, blk), cols]
            s = _dot_nt(q_ref[:, cols], kb) + bias_ref[g, qi - kj]
            m_prev = m_sc[g]
            m_new = jnp.maximum(m_prev, jnp.max(s, axis=-1, keepdims=True))
            p = jnp.exp(s - jnp.concatenate([m_new] * (blk // LANES), axis=1))
            alpha = jnp.exp(m_prev - m_new)
            l_sc[g] = alpha * l_sc[g] + jnp.sum(p, axis=-1, keepdims=True)
            acc_sc[g] = alpha * acc_sc[g] + _dot(p.astype(BF16), vb)
            m_sc[g] = m_new
        return carry

    lax.fori_loop(0, qi + 1, body, 0)
    for g in range(heads):
        o_ref[:, g * e:(g + 1) * e] = (acc_sc[g] / l_sc[g]).astype(o_ref.dtype)


def _dswa_prompt(qb, kb, vb, batch, seq, blk, heads):
    nq = seq // blk
    t = np.arange(blk)
    dist = (np.arange(nq)[:, None, None] * blk + t[None, :, None] - t[None, None, :])
    bias = jnp.asarray(_branch_bias(dist))
    k3 = kb.reshape(batch, seq, DSWA_WIDTH)
    v3 = vb.reshape(batch, seq, DSWA_WIDTH)
    width = heads * DSWA_HEAD_DIM
    q_spec = pl.BlockSpec((blk, width), lambda h, b, i: (b * nq + i, h))
    kv_spec = pl.BlockSpec((1, seq, width), lambda h, b, i: (b, 0, h))
    stat = pltpu.VMEM((heads, blk, DSWA_HEAD_DIM), F32)
    return pl.pallas_call(
        functools.partial(_dswa_kernel, blk=blk, heads=heads),
        grid=(DSWA_HEADS // heads, batch, nq),
        in_specs=[q_spec, kv_spec, kv_spec,
                  pl.BlockSpec((heads, nq, blk, blk), lambda h, b, i: (h, 0, 0, 0),
                               pipeline_mode=pl.Buffered(1))],
        out_specs=q_spec,
        out_shape=jax.ShapeDtypeStruct((batch * seq, DSWA_WIDTH), BF16),
        scratch_shapes=[stat, stat, stat],
        compiler_params=_params(("parallel", "parallel", "arbitrary")),
        name="dswa_prompt",
    )(qb, k3, v3, bias)


def _dswa_sample_kernel(q_ref, ka_ref, va_ref, kb_ref, vb_ref, kn_ref, vn_ref, ba_ref, bb_ref, bn_ref,
                        o_ref):
    e = DSWA_HEAD_DIM
    q = q_ref[0]
    flat = lambda ref_val: ref_val.reshape(-1, e).astype(BF16)
    sa = _dot_nt(q, flat(ka_ref[0, 0])) + ba_ref[...]
    sb = _dot_nt(q, flat(kb_ref[0])) + bb_ref[...]
    sn = _dot_nt(q, kn_ref[0]) + bn_ref[...]
    rowmax = lambda s: jnp.max(s, axis=-1, keepdims=True)
    m = jnp.maximum(jnp.maximum(rowmax(sa), rowmax(sb)), rowmax(sn))
    pa = jnp.exp(sa - m)
    pb = jnp.exp(sb - m)
    pn = jnp.exp(sn - m)
    rowsum = lambda p: jnp.sum(p, axis=-1, keepdims=True)
    den = rowsum(pa) + rowsum(pb) + rowsum(pn)
    out = (_dot(pa.astype(BF16), flat(va_ref[0, 0])) + _dot(pb.astype(BF16), flat(vb_ref[0]))
           + _dot(pn.astype(BF16), vn_ref[0]))
    o_ref[0] = out / den


def _cross_head_bias(bias):
    h, n_new, n_keys = bias.shape
    full = np.full((h, n_new, n_keys, h), NEG_BIG, np.float32)
    for hh in range(h):
        full[hh, :, :, hh] = bias[hh]
    return full.reshape(h * n_new, n_keys * h)


def _dswa_sample(qb, k_new, v_new, cache_k, cache_v):
    batch, n_new, h, e = qb.shape
    n_past = cache_k.shape[1]
    tail = DSWA_BRANCHES[1][0]
    wide_window, wide_dil = DSWA_BRANCHES[2]
    assert DSWA_BRANCHES[0][0] <= tail
    assert n_past % tail == 0 and n_past % wide_dil == 0 and tail % wide_dil == 0 and n_new <= wide_dil
    n_groups = (n_past - tail) // wide_dil
    rows = h * n_new
    lq = np.arange(n_new)
    pos_a = n_past - tail + np.arange(tail)
    pos_b = (wide_dil * np.arange(n_groups)[:, None] + np.arange(n_new)[None, :]).reshape(-1)
    bias_a = _cross_head_bias(_branch_bias(n_past + lq[:, None] - pos_a[None, :]))
    bias_b = _cross_head_bias(_branch_bias(n_past + lq[:, None] - pos_b[None, :]))
    bias_n = np.full((rows, LANES), NEG_BIG, np.float32)
    bias_n[:, :n_new * h] = _cross_head_bias(_branch_bias(lq[:, None] - lq[None, :]))
    q2 = qb.transpose(0, 2, 1, 3).reshape(batch, rows, e)
    pad = ((0, 0), (0, LANES - n_new * h), (0, 0))
    kn = jnp.pad(k_new.reshape(batch, n_new * h, e).astype(BF16), pad)
    vn = jnp.pad(v_new.reshape(batch, n_new * h, e).astype(BF16), pad)
    tail_view = lambda c: c.reshape(batch, n_past // tail, tail, h, e)
    wide_view = lambda c: c.reshape(batch, n_past // wide_dil, wide_dil, h, e)
    tail_spec = pl.BlockSpec((1, 1, tail, h, e), lambda b: (b, n_past // tail - 1, 0, 0, 0))
    wide_spec = pl.BlockSpec((1, n_groups, n_new, h, e), lambda b: (b, 0, 0, 0, 0))
    b3 = lambda b: (b, 0, 0)
    c2 = lambda b: (0, 0)
    out = pl.pallas_call(
        _dswa_sample_kernel,
        grid=(batch,),
        in_specs=[pl.BlockSpec((1, rows, e), b3), tail_spec, tail_spec, wide_spec, wide_spec,
                  pl.BlockSpec((1, LANES, e), b3), pl.BlockSpec((1, LANES, e), b3),
                  pl.BlockSpec(bias_a.shape, c2), pl.BlockSpec(bias_b.shape, c2),
                  pl.BlockSpec(bias_n.shape, c2)],
        out_specs=pl.BlockSpec((1, rows, e), b3),
        out_shape=jax.ShapeDtypeStruct((batch, rows, e), F32),
        compiler_params=_params(("parallel",)),
        name="dswa_sample",
    )(q2, tail_view(cache_k), tail_view(cache_v), wide_view(cache_k), wide_view(cache_v), kn, vn,
      jnp.asarray(bias_a), jnp.asarray(bias_b), jnp.asarray(bias_n))
    return out.reshape(batch, h, n_new, e).transpose(0, 2, 1, 3).reshape(batch, n_new, h * e)


def _layer_norm(pre, g, b):
    mu = jnp.mean(pre, axis=-1, keepdims=True)
    cen = pre - mu
    var = jnp.mean(cen * cen, axis=-1, keepdims=True)
    return cen * lax.rsqrt(var + LN_EPS) * g + b


def _outproj_kernel(ogp_ref, odp_ref, xp_ref, ogs_ref, ods_ref, xs_ref, wo_ref, g_ref, b_ref,
                    wr_ref, br_ref, tri_ref, h_ref, route_ref, cnt_ref, cnt_sc, *, n_prompt_tiles):
    i = pl.program_id(0)
    is_s = i >= n_prompt_tiles

    @pl.when(i == 0)
    def _():
        cnt_sc[...] = jnp.zeros(cnt_sc.shape, F32)

    og = jnp.where(is_s, ogs_ref[...], ogp_ref[...])
    od = jnp.where(is_s, ods_ref[...], odp_ref[...])
    x = jnp.where(is_s, xs_ref[...], xp_ref[...])
    acc = _dot(og, wo_ref[0:GLA_WIDTH, :]) + _dot(od, wo_ref[GLA_WIDTH:, :])
    h = _layer_norm(DEEPNORM_ALPHA * x + acc, g_ref[...], b_ref[...])
    h_ref[...] = h
    logits = _dot(h.astype(BF16), wr_ref[...]) + br_ref[...]
    lane = lax.broadcasted_iota(I32, logits.shape, 1).astype(F32)
    tops, sels, idxs = [], [], []
    cur = logits
    for _ in range(TOP_K):
        mx = jnp.max(cur, axis=-1, keepdims=True)
        idx = jnp.min(jnp.where(cur == mx, lane, float(LANES)), axis=-1, keepdims=True)
        sel = lane == idx
        tops.append(mx)
        sels.append(sel)
        idxs.append(idx)
        cur = jnp.where(sel, -jnp.inf, cur)
    exps = [jnp.exp(t - tops[0]) for t in tops]
    den = exps[0] + exps[1] + exps[2] + exps[3]
    multi = jnp.zeros(logits.shape, F32)
    for sel in sels:
        multi = multi + jnp.where(sel, 1.0, 0.0)
    rank_all = _dot(tri_ref[...], multi.astype(BF16)) + cnt_sc[...]
    cnt_sc[...] = cnt_sc[...] + jnp.sum(multi, axis=0, keepdims=True)
    cnt_ref[...] = cnt_sc[...]
    route = jnp.zeros(logits.shape, F32)
    for kk in range(TOP_K):
        rank = jnp.sum(jnp.where(sels[kk], rank_all, 0.0), axis=-1, keepdims=True)
        route = jnp.where(lane == kk, idxs[kk], route)
        route = jnp.where(lane == TOP_K + kk, exps[kk] / den, route)
        route = jnp.where(lane == 2 * TOP_K + kk, rank, route)
    route_ref[...] = route


def _outproj(og_p, od_p, x_p, og_s, od_s, x_s, wo, g, b, wr, br, tm):
    n_p = x_p.shape[0] // tm
    tri = jnp.asarray(np.tril(np.ones((tm, tm), np.float32), -1), BF16)
    prow = lambda i: (jnp.minimum(i, n_p - 1), 0)
    c2 = lambda i: (0, 0)
    row = lambda i: (i, 0)
    m_tot = x_p.shape[0] + tm
    return pl.pallas_call(
        functools.partial(_outproj_kernel, n_prompt_tiles=n_p),
        grid=(n_p + 1,),
        in_specs=[pl.BlockSpec((tm, GLA_WIDTH), prow), pl.BlockSpec((tm, DSWA_WIDTH), prow),
                  pl.BlockSpec((tm, D_MODEL), prow),
                  pl.BlockSpec((tm, GLA_WIDTH), c2), pl.BlockSpec((tm, DSWA_WIDTH), c2),
                  pl.BlockSpec((tm, D_MODEL), c2),
                  pl.BlockSpec(wo.shape, c2), pl.BlockSpec(g.shape, c2), pl.BlockSpec(b.shape, c2),
                  pl.BlockSpec(wr.shape, c2), pl.BlockSpec(br.shape, c2), pl.BlockSpec(tri.shape, c2)],
        out_specs=[pl.BlockSpec((tm, D_MODEL), row), pl.BlockSpec((tm, LANES), row),
                   pl.BlockSpec((1, LANES), c2)],
        out_shape=[jax.ShapeDtypeStruct((m_tot, D_MODEL), F32),
                   jax.ShapeDtypeStruct((m_tot, LANES), F32),
                   jax.ShapeDtypeStruct((1, LANES), F32)],
        scratch_shapes=[pltpu.VMEM((1, LANES), F32)],
        compiler_params=_params(("arbitrary",)),
        name="outproj_ln_router",
    )(og_p, od_p, x_p, og_s, od_s, x_s, wo, g, b, wr, br, tri)


def _dispatch_kernel(cnt_ref, pst_ref, dest_ref, h_ref, xr_ref, zero_sc, sem, zsem, *, tm, tb, n_blocks):
    i = pl.program_id(0)

    def row_copy(t, r):
        return pltpu.make_async_copy(h_ref.at[pl.ds(t, 1), :], xr_ref.at[pl.ds(r, 1), :], sem)

    def issue(t, carry):
        for kk in range(TOP_K):
            row_copy(t, dest_ref[0, 0, t * TOP_K + kk]).start()
        return carry

    lax.fori_loop(0, tm, issue, 0)

    def zero_copy(r):
        return pltpu.make_async_copy(zero_sc.at[pl.ds(0, 1), :], xr_ref.at[pl.ds(r, 1), :], zsem)

    @pl.when(i == 0)
    def _():
        zero_sc[...] = jnp.zeros(zero_sc.shape, F32)

        def per_expert(e, carry):
            cnt = cnt_ref[e]
            base = pst_ref[e]
            end = (cnt + tb - 1) // tb * tb

            def start(r, c2):
                zero_copy(base + r).start()
                return c2

            def wait(r, c2):
                zero_copy(base + r).wait()
                return c2

            lax.fori_loop(cnt, end, start, 0)
            lax.fori_loop(cnt, end, wait, 0)
            return carry

        lax.fori_loop(0, N_EXPERTS, per_expert, 0)

        last = N_EXPERTS - 1
        n_used = (pst_ref[last] + (cnt_ref[last] + tb - 1) // tb * tb) // tb

        def tail_copy(blk):
            return pltpu.make_async_copy(zero_sc, xr_ref.at[pl.ds(pl.multiple_of(blk * tb, tb), tb), :], zsem)

        def tail_start(blk, c2):
            tail_copy(blk).start()
            return c2

        def tail_wait(blk, c2):
            tail_copy(blk).wait()
            return c2

        lax.fori_loop(n_used, n_blocks, tail_start, 0)
        lax.fori_loop(n_used, n_blocks, tail_wait, 0)

    def drain(t, carry):
        for kk in range(TOP_K):
            row_copy(t, 0).wait()
        return carry

    lax.fori_loop(0, tm, drain, 0)


def _dispatch(h, dest, counts, pstart, n_rows, tm, tb):
    m = h.shape[0]
    n_t = m // tm
    dest3 = dest.reshape(n_t, 1, tm * TOP_K)
    grid_spec = pltpu.PrefetchScalarGridSpec(
        num_scalar_prefetch=2,
        grid=(n_t,),
        in_specs=[pl.BlockSpec((1, 1, tm * TOP_K), lambda i, c, p: (i, 0, 0), memory_space=pltpu.SMEM),
                  pl.BlockSpec((tm, D_MODEL), lambda i, c, p: (i, 0))],
        out_specs=pl.BlockSpec(memory_space=pl.ANY),
        scratch_shapes=[pltpu.VMEM((tb, D_MODEL), F32), pltpu.SemaphoreType.DMA(()),
                        pltpu.SemaphoreType.DMA(())],
    )
    return pl.pallas_call(
        functools.partial(_dispatch_kernel, tm=tm, tb=tb, n_blocks=n_rows // tb),
        grid_spec=grid_spec,
        out_shape=jax.ShapeDtypeStruct((n_rows, D_MODEL), F32),
        compiler_params=_params(("arbitrary",)),
        name="moe_dispatch",
    )(counts, pstart, dest3, h)


def _expert_up_kernel(be_ref, nvb_ref, x_ref, wg_ref, wu_ref, bg_ref, bu_ref, a_ref, wgb_sc, wub_sc):
    i = pl.program_id(1)
    prev = be_ref[jnp.maximum(i - 1, 0)]

    @pl.when((i == 0) | (be_ref[i] != prev))
    def _():
        wgb_sc[...] = wg_ref[0].astype(BF16)
        wub_sc[...] = wu_ref[0].astype(BF16)

    @pl.when(i < nvb_ref[0])
    def _():
        xb = x_ref[...].astype(BF16)
        gate = jnp.minimum(_dot(xb, wgb_sc[...]) + bg_ref[0], SWIGLU_LIMIT)
        up = jnp.clip(_dot(xb, wub_sc[...]) + bu_ref[0], -SWIGLU_LIMIT, SWIGLU_LIMIT)
        a_ref[...] = (gate * jax.nn.sigmoid(SWIGLU_ALPHA * gate) * (up + 1.0)).astype(a_ref.dtype)

    @pl.when(i >= nvb_ref[0])
    def _():
        a_ref[...] = jnp.zeros(a_ref.shape, a_ref.dtype)


def _expert_up(x_rows, blk_e, nvb, wg, wu, bg, bu, tb, tn):
    n_rows = x_rows.shape[0]
    nb = n_rows // tb
    nj = D_FF // tn
    rowblk = lambda j, i, be, nv: (jnp.minimum(i, nv[0] - 1), 0)
    wspec = pl.BlockSpec((1, D_MODEL, tn), lambda j, i, be, nv: (be[i], 0, j))
    bspec = pl.BlockSpec((1, 1, tn), lambda j, i, be, nv: (be[i], 0, j))
    grid_spec = pltpu.PrefetchScalarGridSpec(
        num_scalar_prefetch=2,
        grid=(nj, nb),
        in_specs=[pl.BlockSpec((tb, D_MODEL), rowblk), wspec, wspec, bspec, bspec],
        out_specs=pl.BlockSpec((tb, tn), lambda j, i, be, nv: (i, j)),
        scratch_shapes=[pltpu.VMEM((D_MODEL, tn), BF16), pltpu.VMEM((D_MODEL, tn), BF16)],
    )
    return pl.pallas_call(
        _expert_up_kernel,
        grid_spec=grid_spec,
        out_shape=jax.ShapeDtypeStruct((n_rows, D_FF), BF16),
        compiler_params=_params(("arbitrary", "arbitrary")),
        name="moe_up",
    )(blk_e, nvb, x_rows, wg, wu, bg.reshape(N_EXPERTS, 1, D_FF), bu.reshape(N_EXPERTS, 1, D_FF))


def _expert_down_kernel(be_ref, nvb_ref, a_ref, wd_ref, bd_ref, y_ref, wdb_sc):
    i = pl.program_id(1)
    prev = be_ref[jnp.maximum(i - 1, 0)]

    @pl.when((i == 0) | (be_ref[i] != prev))
    def _():
        wdb_sc[...] = wd_ref[0].astype(BF16)

    @pl.when(i < nvb_ref[0])
    def _():
        y_ref[...] = _dot(a_ref[...], wdb_sc[...]) + bd_ref[0]

    @pl.when(i >= nvb_ref[0])
    def _():
        y_ref[...] = jnp.zeros(y_ref.shape, y_ref.dtype)


def _expert_down(a_rows, blk_e, nvb, wd, bd, tb, tn):
    n_rows = a_rows.shape[0]
    nb = n_rows // tb
    nj = D_MODEL // tn
    grid_spec = pltpu.PrefetchScalarGridSpec(
        num_scalar_prefetch=2,
        grid=(nj, nb),
        in_specs=[pl.BlockSpec((tb, D_FF), lambda j, i, be, nv: (jnp.minimum(i, nv[0] - 1), 0)),
                  pl.BlockSpec((1, D_FF, tn), lambda j, i, be, nv: (be[i], 0, j)),
                  pl.BlockSpec((1, 1, tn), lambda j, i, be, nv: (be[i], 0, j))],
        out_specs=pl.BlockSpec((tb, tn), lambda j, i, be, nv: (i, j)),
        scratch_shapes=[pltpu.VMEM((D_FF, tn), BF16)],
    )
    return pl.pallas_call(
        _expert_down_kernel,
        grid_spec=grid_spec,
        out_shape=jax.ShapeDtypeStruct((n_rows, D_MODEL), F32),
        compiler_params=_params(("arbitrary", "arbitrary")),
        name="moe_down",
    )(blk_e, nvb, a_rows, wd, bd.reshape(N_EXPERTS, 1, D_MODEL))


def _combine_kernel(dcur_ref, dnxt_ref, y_ref, h_ref, gate_ref, g_ref, b_ref, yp_ref, ys_ref,
                    buf, sems, *, tm, n_tiles, n_prompt_tiles):
    i = pl.program_id(0)
    slot = i % 2

    def row_copy(dref, t, kk, sl):
        r = dref[0, 0, t * TOP_K + kk]
        return pltpu.make_async_copy(y_ref.at[pl.ds(r, 1), :], buf.at[sl, kk, pl.ds(t, 1), :], sems.at[sl])

    def issue(dref, sl):
        def body(t, carry):
            for kk in range(TOP_K):
                row_copy(dref, t, kk, sl).start()
            return carry
        lax.fori_loop(0, tm, body, 0)

    @pl.when(i == 0)
    def _():
        issue(dcur_ref, 0)

    @pl.when(i + 1 < n_tiles)
    def _():
        issue(dnxt_ref, 1 - slot)

    def drain(t, carry):
        for kk in range(TOP_K):
            row_copy(dcur_ref, t, kk, slot).wait()
        return carry

    lax.fori_loop(0, tm, drain, 0)
    gate = gate_ref[...]
    f = jnp.zeros((tm, D_MODEL), F32)
    for kk in range(TOP_K):
        f = f + buf[slot, kk] * gate[:, TOP_K + kk:TOP_K + kk + 1]
    y = _layer_norm(DEEPNORM_ALPHA * h_ref[...] + f, g_ref[...], b_ref[...])

    @pl.when(i < n_prompt_tiles)
    def _():
        yp_ref[...] = y

    @pl.when(i >= n_prompt_tiles)
    def _():
        ys_ref[...] = y


def _combine(y_rows, h, route, dest, g, b, n_prompt, n_sample, tm):
    n_p = n_prompt // tm
    n_tiles = n_p + n_sample // tm
    dest3 = dest.reshape(-1, 1, tm * TOP_K)
    smem = functools.partial(pl.BlockSpec, (1, 1, tm * TOP_K), memory_space=pltpu.SMEM)
    c2 = lambda i: (0, 0)
    return pl.pallas_call(
        functools.partial(_combine_kernel, tm=tm, n_tiles=n_tiles, n_prompt_tiles=n_p),
        grid=(n_tiles,),
        in_specs=[smem(lambda i: (i, 0, 0)),
                  smem(lambda i: (jnp.minimum(i + 1, n_tiles - 1), 0, 0)),
                  pl.BlockSpec(memory_space=pl.ANY),
                  pl.BlockSpec((tm, D_MODEL), lambda i: (i, 0)),
                  pl.BlockSpec((tm, LANES), lambda i: (i, 0)),
                  pl.BlockSpec(g.shape, c2), pl.BlockSpec(b.shape, c2)],
        out_specs=[pl.BlockSpec((tm, D_MODEL), lambda i: (jnp.minimum(i, n_p - 1), 0)),
                   pl.BlockSpec((tm, D_MODEL), lambda i: (jnp.maximum(i - n_p, 0), 0))],
        out_shape=[jax.ShapeDtypeStruct((n_prompt, D_MODEL), F32),
                   jax.ShapeDtypeStruct((n_sample, D_MODEL), F32)],
        scratch_shapes=[pltpu.VMEM((2, TOP_K, tm, D_MODEL), F32), pltpu.SemaphoreType.DMA((2,))],
        compiler_params=_params(("arbitrary",)),
        name="moe_combine_ln",
    )(dest3, dest3, y_rows, h, route, g, b)


def _moe_and_norm(h, route, cnt, n_prompt, n_sample, w_exp_gate, b_exp_gate, w_exp_up, b_exp_up,
                  w_exp_down, b_exp_down, ln2_g, ln2_b, tb, tn, out_tm, comb_tm):
    m_tot = h.shape[0]
    top_e = route[:, 0:TOP_K].astype(I32)
    rank = route[:, 2 * TOP_K:3 * TOP_K].astype(I32)
    counts = cnt[0, 0:N_EXPERTS].astype(I32)
    padded = (counts + tb - 1) // tb * tb
    pend = jnp.cumsum(padded)
    pstart = pend - padded
    dest = (pstart[top_e] + rank).reshape(-1)
    nb = (m_tot * TOP_K + N_EXPERTS * (tb - 1) + tb - 1) // tb
    nvb = (pend[-1:] // tb).astype(I32)
    blk_start = jnp.arange(nb, dtype=I32) * tb
    blk_e = jnp.minimum(jnp.sum((blk_start[:, None] >= pend[None, :]).astype(I32), axis=1), N_EXPERTS - 1)
    x_rows = _dispatch(h, dest, counts, pstart.astype(I32), nb * tb, out_tm, tb)
    a_rows = _expert_up(x_rows, blk_e, nvb, w_exp_gate, w_exp_up, b_exp_gate, b_exp_up, tb, tn)
    y_rows = _expert_down(a_rows, blk_e, nvb, w_exp_down, b_exp_down, tb, tn)
    return _combine(y_rows, h, route, dest, ln2_g, ln2_b, n_prompt, n_sample, comb_tm)


def _split_w_in(w_in):
    sizes = (GLA_QK, GLA_QK, GLA_WIDTH, GLA_GATE_RANK, GLA_WIDTH, DSWA_WIDTH, DSWA_WIDTH, DSWA_WIDTH)
    o = np.cumsum((0,) + sizes)
    wg = jnp.concatenate([w_in[:, o[0]:o[3]], w_in[:, o[4]:o[5]]], axis=1).astype(BF16)
    wlr = jnp.pad(w_in[:, o[3]:o[4]], ((0, 0), (0, LANES - GLA_GATE_RANK))).astype(BF16)
    wd = w_in[:, o[5]:o[8]].astype(BF16)
    return wg, wlr, wd


def kernel(x_prompt, x_sample, state_gla, cache_dswa_k, cache_dswa_v, w_in, w_gla_alpha, b_gla_alpha,
           gla_norm_w, w_o, ln1_g, ln1_b, w_router, b_router, w_exp_gate, b_exp_gate, w_exp_up,
           b_exp_up, w_exp_down, b_exp_down, ln2_g, ln2_b):
    bp, sp, _ = x_prompt.shape
    bs, ls, _ = x_sample.shape
    n_prompt, n_sample = bp * sp, bs * ls
    n_past = cache_dswa_k.shape[1]

    wg, wlr, wd = _split_w_in(w_in)
    wa = jnp.pad(w_gla_alpha, ((0, LANES - GLA_GATE_RANK), (0, 0))).astype(BF16)
    ba = b_gla_alpha.reshape(1, GLA_QK)
    nw = gla_norm_w.reshape(1, GLA_WIDTH)
    xp = x_prompt.reshape(n_prompt, D_MODEL)
    xs = x_sample.reshape(n_sample, D_MODEL)

    q, k, v, r, g = _proj_gla(xp, wg, wlr, wa, ba, PROJ_TM)
    og_p, state_p = _gla(q, k, g, v, r, nw, jnp.zeros((bp, GLA_HEADS, GLA_DK, GLA_DV), F32),
                         bp, sp, GLA_ROWS)
    dqb, dk, dv, dkb, dvb = _proj_dswa(xp, wd, PROJ_TM)
    od_p = _dswa_prompt(dqb, dkb, dvb, bp, sp, ATT_T, ATT_HEADS)

    qs, ks, vs, rs, gs = _proj_gla(xs, wg, wlr, wa, ba, n_sample)
    lpad = GLA_CHUNK - ls
    pad_rows = lambda a: jnp.pad(a.reshape(bs, ls, -1), ((0, 0), (0, lpad), (0, 0))).reshape(bs * GLA_CHUNK, -1)
    og_s_pad, state_s = _gla(pad_rows(qs), pad_rows(ks), pad_rows(gs), pad_rows(vs), pad_rows(rs), nw,
                             state_gla, bs, GLA_CHUNK, GLA_CHUNK)
    og_s = og_s_pad.reshape(bs, GLA_CHUNK, GLA_WIDTH)[:, :ls].reshape(n_sample, GLA_WIDTH)
    tqb, tk, tv, _, _ = _proj_dswa(xs, wd, n_sample)
    heads4 = lambda a: a.reshape(bs, ls, DSWA_HEADS, DSWA_HEAD_DIM)
    od_s = _dswa_sample(heads4(tqb), heads4(tk), heads4(tv), cache_dswa_k, cache_dswa_v)
    od_s = od_s.reshape(n_sample, DSWA_WIDTH).astype(BF16)

    spad = ((0, OUT_TM - n_sample), (0, 0))
    wr = jnp.pad(w_router, ((0, 0), (0, LANES - N_EXPERTS))).astype(BF16)
    br = jnp.pad(b_router, (0, LANES - N_EXPERTS), constant_values=NEG_BIG).reshape(1, LANES)
    h, route, cnt = _outproj(og_p, od_p, xp, jnp.pad(og_s, spad), jnp.pad(od_s, spad), jnp.pad(xs, spad),
                             w_o.astype(BF16), ln1_g.reshape(1, D_MODEL), ln1_b.reshape(1, D_MODEL),
                             wr, br, OUT_TM)
    y_p, y_s = _moe_and_norm(h, route, cnt, n_prompt, n_sample, w_exp_gate, b_exp_gate, w_exp_up,
                             b_exp_up, w_exp_down, b_exp_down, ln2_g.reshape(1, D_MODEL),
                             ln2_b.reshape(1, D_MODEL), MOE_TB, MOE_TN, OUT_TM, COMB_TM)

    return (y_p.reshape(bp, sp, D_MODEL), y_s.reshape(bs, ls, D_MODEL), state_p,
            dk.reshape(bp, sp, DSWA_HEADS, DSWA_HEAD_DIM), dv.reshape(bp, sp, DSWA_HEADS, DSWA_HEAD_DIM),
            state_s, tk.reshape(bs, ls, DSWA_HEADS, DSWA_HEAD_DIM),
            tv.reshape(bs, ls, DSWA_HEADS, DSWA_HEAD_DIM))
```

```python
import functools

import numpy as np
import jax
import jax.numpy as jnp
from jax import lax
from jax.experimental import pallas as pl
from jax.experimental.pallas import tpu as pltpu

F32 = jnp.float32
BF16 = jnp.bfloat16
I32 = jnp.int32

D_MODEL = 2048
GLA_HEADS = 4
GLA_DK = 128
GLA_DV = 256
GLA_WIDTH = GLA_HEADS * GLA_DV
GLA_QK = GLA_HEADS * GLA_DK
GLA_GATE_RANK = 16
GLA_TAU = 16.0
GLA_CHUNK = 64
DSWA_HEADS = 8
DSWA_HEAD_DIM = 128
DSWA_WIDTH = DSWA_HEADS * DSWA_HEAD_DIM
DSWA_BRANCHES = ((128, 1), (512, 4), (2048, 16))
N_EXPERTS = 32
TOP_K = 4
D_FF = D_MODEL
SWIGLU_LIMIT = 7.0
SWIGLU_ALPHA = 1.702
DEEPNORM_ALPHA = 2.0 ** 0.25
LN_EPS = 1e-5
RMS_EPS = 1e-6
NEG_BIG = -1e30

LANES = 128
V7X_VMEM_LIMIT = 56 * 1024 * 1024

PROJ_TM = 256
GLA_ROWS = 256
GLA_STEP_HEADS = 2
ATT_T = 256
ATT_HEADS = 4
OUT_TM = 256
MOE_TB = 256
MOE_W_CHUNK = 128
COMB_TM = 128


def _dot(a, b):
    return jnp.dot(a, b, preferred_element_type=F32)


def _dot_nt(a, b):
    return lax.dot_general(a, b, (((1,), (1,)), ((), ())), preferred_element_type=F32)


def _params(sem, vmem=V7X_VMEM_LIMIT):
    return pltpu.CompilerParams(dimension_semantics=sem, vmem_limit_bytes=vmem)


def _proj_gla_kernel(x_ref, wg_ref, wlr_ref, wa_ref, ba_ref, q_ref, k_ref, v_ref, r_ref, g_ref):
    xb = x_ref[...].astype(BF16)
    q_ref[...] = _dot(xb, wg_ref[:, 0:GLA_QK])
    k_ref[...] = _dot(xb, wg_ref[:, GLA_QK:2 * GLA_QK])
    v_ref[...] = _dot(xb, wg_ref[:, 2 * GLA_QK:2 * GLA_QK + GLA_WIDTH]).astype(v_ref.dtype)
    r_ref[...] = _dot(xb, wg_ref[:, 2 * GLA_QK + GLA_WIDTH:])
    lr = _dot(xb, wlr_ref[...])
    z = _dot(lr.astype(BF16), wa_ref[...]) + ba_ref[...]
    log_sig = jnp.minimum(z, 0.0) - jnp.log1p(jnp.exp(-jnp.abs(z)))
    g_ref[...] = log_sig * (1.0 / GLA_TAU)


def _proj_gla(x, wg, wlr, wa, ba, tm):
    m = x.shape[0]
    const = lambda i: (0, 0)
    row = lambda i: (i, 0)
    return pl.pallas_call(
        _proj_gla_kernel,
        grid=(m // tm,),
        in_specs=[pl.BlockSpec((tm, D_MODEL), row),
                  pl.BlockSpec(wg.shape, const),
                  pl.BlockSpec(wlr.shape, const),
                  pl.BlockSpec(wa.shape, const),
                  pl.BlockSpec(ba.shape, const)],
        out_specs=[pl.BlockSpec((tm, GLA_QK), row), pl.BlockSpec((tm, GLA_QK), row),
                   pl.BlockSpec((tm, GLA_WIDTH), row), pl.BlockSpec((tm, GLA_WIDTH), row),
                   pl.BlockSpec((tm, GLA_QK), row)],
        out_shape=[jax.ShapeDtypeStruct((m, GLA_QK), F32), jax.ShapeDtypeStruct((m, GLA_QK), F32),
                   jax.ShapeDtypeStruct((m, GLA_WIDTH), BF16), jax.ShapeDtypeStruct((m, GLA_WIDTH), F32),
                   jax.ShapeDtypeStruct((m, GLA_QK), F32)],
        compiler_params=_params(("parallel",)),
        name="proj_gla",
    )(x, wg, wlr, wa, ba)


def _proj_dswa_kernel(x_ref, w_ref, qb_ref, k_ref, v_ref, kb_ref, vb_ref):
    xb = x_ref[...].astype(BF16)
    w = DSWA_WIDTH
    qb_ref[...] = (_dot(xb, w_ref[:, 0:w]) * (DSWA_HEAD_DIM ** -0.5)).astype(BF16)
    k = _dot(xb, w_ref[:, w:2 * w])
    k_ref[...] = k
    kb_ref[...] = k.astype(BF16)
    v = _dot(xb, w_ref[:, 2 * w:3 * w])
    v_ref[...] = v
    vb_ref[...] = v.astype(BF16)


def _proj_dswa(x, w, tm):
    m = x.shape[0]
    row = lambda i: (i, 0)
    ospec = pl.BlockSpec((tm, DSWA_WIDTH), row)
    return pl.pallas_call(
        _proj_dswa_kernel,
        grid=(m // tm,),
        in_specs=[pl.BlockSpec((tm, D_MODEL), row), pl.BlockSpec(w.shape, lambda i: (0, 0))],
        out_specs=[ospec] * 5,
        out_shape=[jax.ShapeDtypeStruct((m, DSWA_WIDTH), BF16),
                   jax.ShapeDtypeStruct((m, DSWA_WIDTH), F32),
                   jax.ShapeDtypeStruct((m, DSWA_WIDTH), F32),
                   jax.ShapeDtypeStruct((m, DSWA_WIDTH), BF16),
                   jax.ShapeDtypeStruct((m, DSWA_WIDTH), BF16)],
        compiler_params=_params(("parallel",)),
        name="proj_dswa",
    )(x, w)


def _gla_constants():
    c = GLA_CHUNK
    t = np.arange(c)
    tril = (t[None, :] <= t[:, None]).astype(np.float32)
    rows = [t, np.full(c, 31), 32 * (t // 32) + 15, 16 * (t // 16) + 7, np.full(c, c - 1)]
    cum = np.concatenate([tril[r] for r in rows], axis=0)

    def level_mask(half):
        blk = 2 * half
        return ((t[:, None] // blk == t[None, :] // blk) & (t[:, None] % blk >= half)
                & (t[None, :] % blk < half)).astype(np.float32)

    lmask = np.stack([level_mask(32), level_mask(16), level_mask(8)])
    dmask = np.stack([((t[None, :] == 8 * (t[:, None] // 8) + s) & (t[:, None] % 8 >= s)).astype(np.float32)
                      for s in range(8)])
    return jnp.asarray(cum, BF16), jnp.asarray(lmask), jnp.asarray(dmask)


def _gla_kernel(q_ref, k_ref, g_ref, v_ref, r_ref, nw_ref, s0_ref, cum_ref, lmask_ref, dmask_ref,
                o_ref, sout_ref, s_sc, *, n_chunks, n_steps, heads):
    step = pl.program_id(2)
    c = GLA_CHUNK

    @pl.when(step == 0)
    def _():
        s_sc[...] = s0_ref[0]

    for ci, hh in [(ci, hh) for ci in range(n_chunks) for hh in range(heads)]:
        rows = slice(ci * c, (ci + 1) * c)
        kcols = slice(hh * GLA_DK, (hh + 1) * GLA_DK)
        vcols = slice(hh * GLA_DV, (hh + 1) * GLA_DV)
        q = q_ref[rows, kcols] * (GLA_DK ** -0.5)
        k = k_ref[rows, kcols]
        g = g_ref[rows, kcols]
        v = v_ref[rows, vcols]
        g1 = g.astype(BF16)
        e1 = g - g1.astype(F32)
        g2 = e1.astype(BF16)
        g3 = (e1 - g2.astype(F32)).astype(BF16)
        cb = _dot(cum_ref[...], jnp.concatenate([g1, g2, g3], axis=1))
        cb = cb[:, 0:GLA_DK] + cb[:, GLA_DK:2 * GLA_DK] + cb[:, 2 * GLA_DK:3 * GLA_DK]
        b = cb[0:c]
        b_end = cb[4 * c:5 * c]
        state = s_sc[hh]
        o = _dot((q * jnp.exp(b)).astype(BF16), state.astype(BF16))
        att = jnp.zeros((c, c), F32)
        for li in range(3):
            bl = cb[(li + 1) * c:(li + 2) * c]
            ql = (q * jnp.exp(jnp.minimum(b - bl, 0.0))).astype(BF16)
            kl = (k * jnp.exp(jnp.minimum(bl - b, 0.0))).astype(BF16)
            att = att + _dot_nt(ql, kl) * lmask_ref[li]
        k3 = k.reshape(c // 8, 8, GLA_DK)
        b3 = b.reshape(c // 8, 8, GLA_DK)
        for sl in range(8):
            kb = jnp.broadcast_to(k3[:, sl:sl + 1, :], k3.shape).reshape(c, GLA_DK)
            bb = jnp.broadcast_to(b3[:, sl:sl + 1, :], b3.shape).reshape(c, GLA_DK)
            d = q * kb * jnp.exp(jnp.minimum(b - bb, 0.0))
            att = att + jnp.sum(d, axis=-1, keepdims=True) * dmask_ref[sl]
        o = o + _dot(att.astype(BF16), v)
        kd_t = (k * jnp.exp(b_end - b)).T.astype(BF16)
        e_col = jnp.exp(b_end).T[:, 0:1]
        s_sc[hh] = state * e_col + _dot(kd_t, v)
        on = o * lax.rsqrt(jnp.mean(o * o, axis=-1, keepdims=True) + RMS_EPS) * nw_ref[:, vcols]
        r = r_ref[rows, vcols]
        o_ref[rows, vcols] = (on * (r * jax.nn.sigmoid(r))).astype(o_ref.dtype)

    @pl.when(step == n_steps - 1)
    def _():
        sout_ref[0] = s_sc[...]


def _gla(q, k, g, v, r, nw, s0, batch, length, rows_per_step, heads):
    n_steps = length // rows_per_step
    cum, lmask, dmask = _gla_constants()
    qk_spec = pl.BlockSpec((rows_per_step, heads * GLA_DK), lambda b, h, t: (b * n_steps + t, h))
    v_spec = pl.BlockSpec((rows_per_step, heads * GLA_DV), lambda b, h, t: (b * n_steps + t, h))
    s_spec = pl.BlockSpec((1, heads, GLA_DK, GLA_DV), lambda b, h, t: (b, h, 0, 0))
    const2 = lambda b, h, t: (0, 0)
    const3 = lambda b, h, t: (0, 0, 0)
    kern = functools.partial(_gla_kernel, n_chunks=rows_per_step // GLA_CHUNK, n_steps=n_steps, heads=heads)
    return pl.pallas_call(
        kern,
        grid=(batch, GLA_HEADS // heads, n_steps),
        in_specs=[qk_spec, qk_spec, qk_spec, v_spec, v_spec,
                  pl.BlockSpec((1, heads * GLA_DV), lambda b, h, t: (0, h)),
                  s_spec,
                  pl.BlockSpec(cum.shape, const2),
                  pl.BlockSpec(lmask.shape, const3),
                  pl.BlockSpec(dmask.shape, const3)],
        out_specs=[v_spec, s_spec],
        out_shape=[jax.ShapeDtypeStruct((batch * length, GLA_WIDTH), BF16),
                   jax.ShapeDtypeStruct((batch, GLA_HEADS, GLA_DK, GLA_DV), F32)],
        scratch_shapes=[pltpu.VMEM((heads, GLA_DK, GLA_DV), F32)],
        compiler_params=_params(("parallel", "parallel", "arbitrary")),
        name="gla",
    )(q, k, g, v, r, nw, s0, cum, lmask, dmask)


def _alibi_slopes():
    return 2.0 ** (-8.0 * (np.arange(DSWA_HEADS, dtype=np.float64) + 1.0) / DSWA_HEADS)


def _branch_bias(dist):
    dist = np.asarray(dist, np.int64)
    mult = np.zeros(dist.shape, np.float64)
    for window, dil in DSWA_BRANCHES:
        mult += (dist >= 0) & (dist <= window) & (dist % dil == 0)
    slopes = _alibi_slopes().reshape((-1,) + (1,) * dist.ndim)
    bias = np.where(mult > 0, np.log(np.maximum(mult, 1.0)) - slopes * dist, NEG_BIG)
    return bias.astype(np.float32)


def _dswa_kernel(q_ref, k_ref, v_ref, bias_ref, o_ref, m_sc, l_sc, acc_sc, *, blk, heads):
    qi = pl.program_id(2)
    e = DSWA_HEAD_DIM
    m_sc[...] = jnp.full(m_sc.shape, NEG_BIG, F32)
    l_sc[...] = jnp.zeros(l_sc.shape, F32)
    acc_sc[...] = jnp.zeros(acc_sc.shape, F32)

    def body(kj, carry):
        off = pl.multiple_of(kj * blk, blk)
        for g in range(heads):
            cols = slice(g * e, (g + 1) * e)
            kb = k_ref[0, pl.ds(off, blk), cols]
            vb = v_ref[0, pl.ds(off, blk), cols]
            s = _dot_nt(q_ref[:, cols], kb) + bias_ref[g, qi - kj]
            m_prev = m_sc[g]
            m_new = jnp.maximum(m_prev, jnp.max(s, axis=-1, keepdims=True))
            p = jnp.exp(s - jnp.concatenate([m_new] * (blk // LANES), axis=1))
            alpha = jnp.exp(m_prev - m_new)
            l_sc[g] = alpha * l_sc[g] + jnp.sum(p, axis=-1, keepdims=True)
            acc_sc[g] = alpha * acc_sc[g] + _dot(p.astype(BF16), vb)
            m_sc[g] = m_new
        return carry

    lax.fori_loop(0, qi + 1, body, 0)
    for g in range(heads):
        o_ref[:, g * e:(g + 1) * e] = (acc_sc[g] / l_sc[g]).astype(o_ref.dtype)


def _dswa_prompt(qb, kb, vb, batch, seq, blk, heads):
    nq = seq // blk
    t = np.arange(blk)
    dist = (np.arange(nq)[:, None, None] * blk + t[None, :, None] - t[None, None, :])
    bias = jnp.asarray(_branch_bias(dist))
    k3 = kb.reshape(batch, seq, DSWA_WIDTH)
    v3 = vb.reshape(batch, seq, DSWA_WIDTH)
    width = heads * DSWA_HEAD_DIM
    q_spec = pl.BlockSpec((blk, width), lambda h, b, i: (b * nq + i, h))
    kv_spec = pl.BlockSpec((1, seq, width), lambda h, b, i: (b, 0, h))
    stat = pltpu.VMEM((heads, blk, DSWA_HEAD_DIM), F32)
    return pl.pallas_call(
        functools.partial(_dswa_kernel, blk=blk, heads=heads),
        grid=(DSWA_HEADS // heads, batch, nq),
        in_specs=[q_spec, kv_spec, kv_spec,
                  pl.BlockSpec((heads, nq, blk, blk), lambda h, b, i: (h, 0, 0, 0),
                               pipeline_mode=pl.Buffered(1))],
        out_specs=q_spec,
        out_shape=jax.ShapeDtypeStruct((batch * seq, DSWA_WIDTH), BF16),
        scratch_shapes=[stat, stat, stat],
        compiler_params=_params(("parallel", "parallel", "arbitrary")),
        name="dswa_prompt",
    )(qb, k3, v3, bias)


def _dswa_sample_kernel(q_ref, ka_ref, va_ref, kb_ref, vb_ref, kn_ref, vn_ref, ba_ref, bb_ref, bn_ref,
                        o_ref):
    e = DSWA_HEAD_DIM
    q = q_ref[0]
    flat = lambda ref_val: ref_val.reshape(-1, e).astype(BF16)
    sa = _dot_nt(q, flat(ka_ref[0, 0])) + ba_ref[...]
    sb = _dot_nt(q, flat(kb_ref[0])) + bb_ref[...]
    sn = _dot_nt(q, kn_ref[0]) + bn_ref[...]
    rowmax = lambda s: jnp.max(s, axis=-1, keepdims=True)
    m = jnp.maximum(jnp.maximum(rowmax(sa), rowmax(sb)), rowmax(sn))
    pa = jnp.exp(sa - m)
    pb = jnp.exp(sb - m)
    pn = jnp.exp(sn - m)
    rowsum = lambda p: jnp.sum(p, axis=-1, keepdims=True)
    den = rowsum(pa) + rowsum(pb) + rowsum(pn)
    out = (_dot(pa.astype(BF16), flat(va_ref[0, 0])) + _dot(pb.astype(BF16), flat(vb_ref[0]))
           + _dot(pn.astype(BF16), vn_ref[0]))
    o_ref[0] = out / den


def _cross_head_bias(bias):
    h, n_new, n_keys = bias.shape
    full = np.full((h, n_new, n_keys, h), NEG_BIG, np.float32)
    for hh in range(h):
        full[hh, :, :, hh] = bias[hh]
    return full.reshape(h * n_new, n_keys * h)


def _dswa_sample(qb, k_new, v_new, cache_k, cache_v):
    batch, n_new, h, e = qb.shape
    n_past = cache_k.shape[1]
    tail = DSWA_BRANCHES[1][0]
    wide_window, wide_dil = DSWA_BRANCHES[2]
    assert DSWA_BRANCHES[0][0] <= tail
    assert n_past % tail == 0 and n_past % wide_dil == 0 and tail % wide_dil == 0 and n_new <= wide_dil
    n_groups = (n_past - tail) // wide_dil
    rows = h * n_new
    lq = np.arange(n_new)
    pos_a = n_past - tail + np.arange(tail)
    pos_b = (wide_dil * np.arange(n_groups)[:, None] + np.arange(n_new)[None, :]).reshape(-1)
    bias_a = _cross_head_bias(_branch_bias(n_past + lq[:, None] - pos_a[None, :]))
    bias_b = _cross_head_bias(_branch_bias(n_past + lq[:, None] - pos_b[None, :]))
    bias_n = np.full((rows, LANES), NEG_BIG, np.float32)
    bias_n[:, :n_new * h] = _cross_head_bias(_branch_bias(lq[:, None] - lq[None, :]))
    q2 = qb.transpose(0, 2, 1, 3).reshape(batch, rows, e)
    pad = ((0, 0), (0, LANES - n_new * h), (0, 0))
    kn = jnp.pad(k_new.reshape(batch, n_new * h, e).astype(BF16), pad)
    vn = jnp.pad(v_new.reshape(batch, n_new * h, e).astype(BF16), pad)
    tail_view = lambda c: c.reshape(batch, n_past // tail, tail, h, e)
    wide_view = lambda c: c.reshape(batch, n_past // wide_dil, wide_dil, h, e)
    tail_spec = pl.BlockSpec((1, 1, tail, h, e), lambda b: (b, n_past // tail - 1, 0, 0, 0))
    wide_spec = pl.BlockSpec((1, n_groups, n_new, h, e), lambda b: (b, 0, 0, 0, 0))
    b3 = lambda b: (b, 0, 0)
    c2 = lambda b: (0, 0)
    out = pl.pallas_call(
        _dswa_sample_kernel,
        grid=(batch,),
        in_specs=[pl.BlockSpec((1, rows, e), b3), tail_spec, tail_spec, wide_spec, wide_spec,
                  pl.BlockSpec((1, LANES, e), b3), pl.BlockSpec((1, LANES, e), b3),
                  pl.BlockSpec(bias_a.shape, c2), pl.BlockSpec(bias_b.shape, c2),
                  pl.BlockSpec(bias_n.shape, c2)],
        out_specs=pl.BlockSpec((1, rows, e), b3),
        out_shape=jax.ShapeDtypeStruct((batch, rows, e), F32),
        compiler_params=_params(("parallel",)),
        name="dswa_sample",
    )(q2, tail_view(cache_k), tail_view(cache_v), wide_view(cache_k), wide_view(cache_v), kn, vn,
      jnp.asarray(bias_a), jnp.asarray(bias_b), jnp.asarray(bias_n))
    return out.reshape(batch, h, n_new, e).transpose(0, 2, 1, 3).reshape(batch, n_new, h * e)


def _layer_norm(pre, g, b):
    mu = jnp.mean(pre, axis=-1, keepdims=True)
    cen = pre - mu
    var = jnp.mean(cen * cen, axis=-1, keepdims=True)
    return cen * lax.rsqrt(var + LN_EPS) * g + b


def _outproj_kernel(ogp_ref, odp_ref, xp_ref, ogs_ref, ods_ref, xs_ref, wo_ref, g_ref, b_ref,
                    wr_ref, br_ref, tri_ref, h_ref, route_ref, cnt_ref, cnt_sc, *, n_prompt_tiles):
    i = pl.program_id(0)
    is_s = i >= n_prompt_tiles

    @pl.when(i == 0)
    def _():
        cnt_sc[...] = jnp.zeros(cnt_sc.shape, F32)

    og = jnp.where(is_s, ogs_ref[...], ogp_ref[...])
    od = jnp.where(is_s, ods_ref[...], odp_ref[...])
    x = jnp.where(is_s, xs_ref[...], xp_ref[...])
    acc = _dot(og, wo_ref[0:GLA_WIDTH, :]) + _dot(od, wo_ref[GLA_WIDTH:, :])
    h = _layer_norm(DEEPNORM_ALPHA * x + acc, g_ref[...], b_ref[...])
    h_ref[...] = h
    logits = _dot(h.astype(BF16), wr_ref[...]) + br_ref[...]
    lane = lax.broadcasted_iota(I32, logits.shape, 1).astype(F32)
    tops, sels, idxs = [], [], []
    cur = logits
    for _ in range(TOP_K):
        mx = jnp.max(cur, axis=-1, keepdims=True)
        idx = jnp.min(jnp.where(cur == mx, lane, float(LANES)), axis=-1, keepdims=True)
        sel = lane == idx
        tops.append(mx)
        sels.append(sel)
        idxs.append(idx)
        cur = jnp.where(sel, -jnp.inf, cur)
    exps = [jnp.exp(t - tops[0]) for t in tops]
    den = exps[0] + exps[1] + exps[2] + exps[3]
    multi = jnp.zeros(logits.shape, F32)
    for sel in sels:
        multi = multi + jnp.where(sel, 1.0, 0.0)
    rank_all = _dot(tri_ref[...], multi.astype(BF16)) + cnt_sc[...]
    cnt_sc[...] = cnt_sc[...] + jnp.sum(multi, axis=0, keepdims=True)
    cnt_ref[...] = cnt_sc[...]
    route = jnp.zeros(logits.shape, F32)
    for kk in range(TOP_K):
        rank = jnp.sum(jnp.where(sels[kk], rank_all, 0.0), axis=-1, keepdims=True)
        route = jnp.where(lane == kk, idxs[kk], route)
        route = jnp.where(lane == TOP_K + kk, exps[kk] / den, route)
        route = jnp.where(lane == 2 * TOP_K + kk, rank, route)
    route_ref[...] = route


def _outproj(og_p, od_p, x_p, og_s, od_s, x_s, wo, g, b, wr, br, tm):
    n_p = x_p.shape[0] // tm
    tri = jnp.asarray(np.tril(np.ones((tm, tm), np.float32), -1), BF16)
    prow = lambda i: (jnp.minimum(i, n_p - 1), 0)
    c2 = lambda i: (0, 0)
    row = lambda i: (i, 0)
    m_tot = x_p.shape[0] + tm
    return pl.pallas_call(
        functools.partial(_outproj_kernel, n_prompt_tiles=n_p),
        grid=(n_p + 1,),
        in_specs=[pl.BlockSpec((tm, GLA_WIDTH), prow), pl.BlockSpec((tm, DSWA_WIDTH), prow),
                  pl.BlockSpec((tm, D_MODEL), prow),
                  pl.BlockSpec((tm, GLA_WIDTH), c2), pl.BlockSpec((tm, DSWA_WIDTH), c2),
                  pl.BlockSpec((tm, D_MODEL), c2),
                  pl.BlockSpec(wo.shape, c2), pl.BlockSpec(g.shape, c2), pl.BlockSpec(b.shape, c2),
                  pl.BlockSpec(wr.shape, c2), pl.BlockSpec(br.shape, c2), pl.BlockSpec(tri.shape, c2)],
        out_specs=[pl.BlockSpec((tm, D_MODEL), row), pl.BlockSpec((tm, LANES), row),
                   pl.BlockSpec((1, LANES), c2)],
        out_shape=[jax.ShapeDtypeStruct((m_tot, D_MODEL), F32),
                   jax.ShapeDtypeStruct((m_tot, LANES), F32),
                   jax.ShapeDtypeStruct((1, LANES), F32)],
        scratch_shapes=[pltpu.VMEM((1, LANES), F32)],
        compiler_params=_params(("arbitrary",)),
        name="outproj_ln_router",
    )(og_p, od_p, x_p, og_s, od_s, x_s, wo, g, b, wr, br, tri)


def _dispatch_kernel(cnt_ref, pst_ref, dest_ref, h_hbm, xr_ref, hbuf, zero_sc, lsems, sems, zsem, *,
                     tm, tb, n_blocks, n_steps):
    i = pl.program_id(0)
    slot = i % 2
    ring = i % 3

    def tile_load(step):
        rows = pl.ds(pl.multiple_of(step * tm, tm), tm)
        return pltpu.make_async_copy(h_hbm.at[rows, :], hbuf.at[step % 3], lsems.at[step % 3])

    @pl.when(i == 0)
    def _():
        tile_load(0).start()

    @pl.when(i + 1 < n_steps)
    def _():
        tile_load(i + 1).start()

    tile_load(i).wait()
    h_ref = hbuf.at[ring]

    def row_copy(t, r, s):
        return pltpu.make_async_copy(h_ref.at[pl.ds(t, 1), :], xr_ref.at[pl.ds(r, 1), :], sems.at[s])

    def issue(t, carry):
        for kk in range(TOP_K):
            row_copy(t, dest_ref[0, 0, t * TOP_K + kk], slot).start()
        return carry

    lax.fori_loop(0, tm, issue, 0)

    def zero_copy(r):
        return pltpu.make_async_copy(zero_sc.at[pl.ds(0, 1), :], xr_ref.at[pl.ds(r, 1), :], zsem)

    @pl.when(i == 0)
    def _():
        zero_sc[...] = jnp.zeros(zero_sc.shape, F32)

        def per_expert(e, carry):
            cnt = cnt_ref[e]
            base = pst_ref[e]
            end = (cnt + tb - 1) // tb * tb

            def start(r, c2):
                zero_copy(base + r).start()
                return c2

            def wait(r, c2):
                zero_copy(base + r).wait()
                return c2

            lax.fori_loop(cnt, end, start, 0)
            lax.fori_loop(cnt, end, wait, 0)
            return carry

        lax.fori_loop(0, N_EXPERTS, per_expert, 0)

        last = N_EXPERTS - 1
        n_used = (pst_ref[last] + (cnt_ref[last] + tb - 1) // tb * tb) // tb

        def tail_copy(blk):
            return pltpu.make_async_copy(zero_sc, xr_ref.at[pl.ds(pl.multiple_of(blk * tb, tb), tb), :], zsem)

        def tail_start(blk, c2):
            tail_copy(blk).start()
            return c2

        def tail_wait(blk, c2):
            tail_copy(blk).wait()
            return c2

        lax.fori_loop(n_used, n_blocks, tail_start, 0)
        lax.fori_loop(n_used, n_blocks, tail_wait, 0)

    def drain(s):
        def body(t, carry):
            for kk in range(TOP_K):
                row_copy(t, 0, s).wait()
            return carry
        lax.fori_loop(0, tm, body, 0)

    @pl.when(i > 0)
    def _():
        drain(1 - slot)

    @pl.when(i == n_steps - 1)
    def _():
        drain(slot)


def _dispatch(h, dest, counts, pstart, n_rows, tm, tb):
    m = h.shape[0]
    n_t = m // tm
    dest3 = dest.reshape(n_t, 1, tm * TOP_K)
    grid_spec = pltpu.PrefetchScalarGridSpec(
        num_scalar_prefetch=2,
        grid=(n_t,),
        in_specs=[pl.BlockSpec((1, 1, tm * TOP_K), lambda i, c, p: (i, 0, 0), memory_space=pltpu.SMEM),
                  pl.BlockSpec(memory_space=pl.ANY)],
        out_specs=pl.BlockSpec(memory_space=pl.ANY),
        scratch_shapes=[pltpu.VMEM((3, tm, D_MODEL), F32), pltpu.VMEM((tb, D_MODEL), F32),
                        pltpu.SemaphoreType.DMA((3,)), pltpu.SemaphoreType.DMA((2,)),
                        pltpu.SemaphoreType.DMA(())],
    )
    return pl.pallas_call(
        functools.partial(_dispatch_kernel, tm=tm, tb=tb, n_blocks=n_rows // tb, n_steps=n_t),
        grid_spec=grid_spec,
        out_shape=jax.ShapeDtypeStruct((n_rows, D_MODEL), F32),
        compiler_params=_params(("arbitrary",)),
        name="moe_dispatch",
    )(counts, pstart, dest3, h)


def _staged_expert_kernel(be_ref, nvb_ref, lo_ref, hi_ref, kick_ref, nxt_ref, par_ref, x_ref, *rest,
                          n_mats, compute):
    w_refs = rest[:n_mats]
    b_refs = rest[n_mats:2 * n_mats]
    o_ref = rest[2 * n_mats]
    wb, stage, sems = rest[2 * n_mats + 1:]
    i = pl.program_id(0)
    n_chunks = wb.shape[2] // MOE_W_CHUNK

    def chunk_rows(c):
        return pl.ds(pl.multiple_of(c * MOE_W_CHUNK, MOE_W_CHUNK), MOE_W_CHUNK)

    def copies(e, c):
        slot = c % 2
        return [pltpu.make_async_copy(w.at[e, chunk_rows(c), :], stage.at[slot, mi], sems.at[slot])
                for mi, w in enumerate(w_refs)]

    def start(e, c):
        for cp in copies(e, c):
            cp.start()

    def finish_range(e, lo, hi, dst):
        def body(c, carry):
            for cp in copies(e, c):
                cp.wait()
            for mi in range(n_mats):
                wb[dst, mi, chunk_rows(c), :] = stage[c % 2, mi].astype(BF16)

            @pl.when(c + 2 < n_chunks)
            def _():
                start(e, c + 2)

            return carry

        lax.fori_loop(lo, hi, body, 0)

    cur = par_ref[i]

    @pl.when(i == 0)
    def _():
        start(be_ref[0], 0)
        start(be_ref[0], 1)
        finish_range(be_ref[0], 0, n_chunks, cur)

    @pl.when(kick_ref[i] == 1)
    def _():
        start(nxt_ref[i], 0)
        start(nxt_ref[i], 1)

    @pl.when(i < nvb_ref[0])
    def _():
        compute(x_ref, [wb.at[cur, mi] for mi in range(n_mats)], b_refs, o_ref)

    @pl.when(i >= nvb_ref[0])
    def _():
        o_ref[...] = jnp.zeros(o_ref.shape, o_ref.dtype)

    finish_range(nxt_ref[i], lo_ref[i], hi_ref[i], 1 - cur)


def _up_compute(x_ref, w, b_refs, a_ref):
    xb = x_ref[...].astype(BF16)
    half = a_ref.shape[1] // 2
    for n in range(2):
        cols = slice(n * half, (n + 1) * half)
        gate = jnp.minimum(_dot(xb, w[0][:, cols]) + b_refs[0][0, :, cols], SWIGLU_LIMIT)
        up = jnp.clip(_dot(xb, w[1][:, cols]) + b_refs[1][0, :, cols], -SWIGLU_LIMIT, SWIGLU_LIMIT)
        a_ref[:, cols] = (gate * jax.nn.sigmoid(SWIGLU_ALPHA * gate) * (up + 1.0)).astype(a_ref.dtype)


def _down_compute(a_ref, w, b_refs, y_ref):
    ab = a_ref[...]
    half = y_ref.shape[1] // 2
    for n in range(2):
        cols = slice(n * half, (n + 1) * half)
        y_ref[:, cols] = _dot(ab, w[0][:, cols]) + b_refs[0][0, :, cols]


def _staged_expert_call(x_rows, meta, weights, biases, out_dtype, compute, name, tb):
    n_rows, k = x_rows.shape
    n_out = weights[0].shape[2]
    n_mats = len(weights)
    rows_map = lambda i, be, nv, *_: (jnp.minimum(i, nv[0] - 1), 0)
    bias_spec = pl.BlockSpec((1, 1, n_out), lambda i, be, *_: (be[i], 0, 0))
    grid_spec = pltpu.PrefetchScalarGridSpec(
        num_scalar_prefetch=len(meta),
        grid=(n_rows // tb,),
        in_specs=([pl.BlockSpec((tb, k), rows_map)] + [pl.BlockSpec(memory_space=pl.ANY)] * n_mats
                  + [bias_spec] * n_mats),
        out_specs=pl.BlockSpec((tb, n_out), lambda i, *_: (i, 0)),
        scratch_shapes=[pltpu.VMEM((2, n_mats, k, n_out), BF16),
                        pltpu.VMEM((2, n_mats, MOE_W_CHUNK, n_out), F32),
                        pltpu.SemaphoreType.DMA((2,))],
    )
    return pl.pallas_call(
        functools.partial(_staged_expert_kernel, n_mats=n_mats, compute=compute),
        grid_spec=grid_spec,
        out_shape=jax.ShapeDtypeStruct((n_rows, n_out), out_dtype),
        compiler_params=_params(("arbitrary",)),
        name=name,
    )(*meta, x_rows, *weights, *[b.reshape(N_EXPERTS, 1, n_out) for b in biases])


def _expert_schedule(blk_e, nvb, padded, pstart, tb, n_chunks):
    nb = blk_e.shape[0]
    ids = jnp.arange(N_EXPERTS, dtype=I32)
    n_blk = padded // tb
    has = n_blk > 0
    later = jnp.where(has[None, :] & (ids[None, :] > ids[:, None]), ids[None, :], N_EXPERTS)
    nxt_e = jnp.min(later, axis=1)
    nxt_e = jnp.where(nxt_e == N_EXPERTS, -1, nxt_e)
    par_e = jnp.maximum(jnp.cumsum(has.astype(I32)) - 1, 0) % 2
    bi = jnp.arange(nb, dtype=I32)
    rank = bi - (pstart // tb)[blk_e]
    n_e = jnp.maximum(n_blk[blk_e], 1)
    active = (bi < nvb[0]) & (nxt_e[blk_e] >= 0)
    lo = jnp.where(active, n_chunks * rank // n_e, 0)
    hi = jnp.where(active, n_chunks * (rank + 1) // n_e, 0)
    kick = (active & (rank == 0)).astype(I32)
    to_i32 = lambda a: a.astype(I32)
    return tuple(map(to_i32, (blk_e, nvb, lo, hi, kick, nxt_e[blk_e], par_e[blk_e])))


def _combine_kernel(dcur_ref, dnxt_ref, y_ref, h_ref, gate_ref, g_ref, b_ref, yp_ref, ys_ref,
                    buf, sems, *, tm, n_tiles, n_prompt_tiles):
    i = pl.program_id(0)
    slot = i % 2

    def row_copy(dref, t, kk, sl):
        r = dref[0, 0, t * TOP_K + kk]
        return pltpu.make_async_copy(y_ref.at[pl.ds(r, 1), :], buf.at[sl, kk, pl.ds(t, 1), :], sems.at[sl])

    def issue(dref, sl):
        def body(t, carry):
            for kk in range(TOP_K):
                row_copy(dref, t, kk, sl).start()
            return carry
        lax.fori_loop(0, tm, body, 0)

    @pl.when(i == 0)
    def _():
        issue(dcur_ref, 0)

    @pl.when(i + 1 < n_tiles)
    def _():
        issue(dnxt_ref, 1 - slot)

    def drain(t, carry):
        for kk in range(TOP_K):
            row_copy(dcur_ref, t, kk, slot).wait()
        return carry

    lax.fori_loop(0, tm, drain, 0)
    gate = gate_ref[...]
    f = jnp.zeros((tm, D_MODEL), F32)
    for kk in range(TOP_K):
        f = f + buf[slot, kk] * gate[:, TOP_K + kk:TOP_K + kk + 1]
    y = _layer_norm(DEEPNORM_ALPHA * h_ref[...] + f, g_ref[...], b_ref[...])

    @pl.when(i < n_prompt_tiles)
    def _():
        yp_ref[...] = y

    @pl.when(i >= n_prompt_tiles)
    def _():
        ys_ref[...] = y


def _combine(y_rows, h, route, dest, g, b, n_prompt, n_sample, tm):
    n_p = n_prompt // tm
    n_tiles = n_p + n_sample // tm
    dest3 = dest.reshape(-1, 1, tm * TOP_K)
    smem = functools.partial(pl.BlockSpec, (1, 1, tm * TOP_K), memory_space=pltpu.SMEM)
    c2 = lambda i: (0, 0)
    return pl.pallas_call(
        functools.partial(_combine_kernel, tm=tm, n_tiles=n_tiles, n_prompt_tiles=n_p),
        grid=(n_tiles,),
        in_specs=[smem(lambda i: (i, 0, 0)),
                  smem(lambda i: (jnp.minimum(i + 1, n_tiles - 1), 0, 0)),
                  pl.BlockSpec(memory_space=pl.ANY),
                  pl.BlockSpec((tm, D_MODEL), lambda i: (i, 0)),
                  pl.BlockSpec((tm, LANES), lambda i: (i, 0)),
                  pl.BlockSpec(g.shape, c2), pl.BlockSpec(b.shape, c2)],
        out_specs=[pl.BlockSpec((tm, D_MODEL), lambda i: (jnp.minimum(i, n_p - 1), 0)),
                   pl.BlockSpec((tm, D_MODEL), lambda i: (jnp.maximum(i - n_p, 0), 0))],
        out_shape=[jax.ShapeDtypeStruct((n_prompt, D_MODEL), F32),
                   jax.ShapeDtypeStruct((n_sample, D_MODEL), F32)],
        scratch_shapes=[pltpu.VMEM((2, TOP_K, tm, D_MODEL), F32), pltpu.SemaphoreType.DMA((2,))],
        compiler_params=_params(("arbitrary",)),
        name="moe_combine_ln",
    )(dest3, dest3, y_rows, h, route, g, b)


def _moe_and_norm(h, route, cnt, n_prompt, n_sample, w_exp_gate, b_exp_gate, w_exp_up, b_exp_up,
                  w_exp_down, b_exp_down, ln2_g, ln2_b, tb, out_tm, comb_tm):
    m_tot = h.shape[0]
    top_e = route[:, 0:TOP_K].astype(I32)
    rank = route[:, 2 * TOP_K:3 * TOP_K].astype(I32)
    counts = cnt[0, 0:N_EXPERTS].astype(I32)
    padded = (counts + tb - 1) // tb * tb
    pend = jnp.cumsum(padded)
    pstart = pend - padded
    dest = (pstart[top_e] + rank).reshape(-1)
    nb = (m_tot * TOP_K + N_EXPERTS * (tb - 1) + tb - 1) // tb
    nvb = (pend[-1:] // tb).astype(I32)
    blk_start = jnp.arange(nb, dtype=I32) * tb
    blk_e = jnp.minimum(jnp.sum((blk_start[:, None] >= pend[None, :]).astype(I32), axis=1), N_EXPERTS - 1)
    meta = _expert_schedule(blk_e, nvb, padded, pstart, tb, D_MODEL // MOE_W_CHUNK)
    x_rows = _dispatch(h, dest, counts, pstart.astype(I32), nb * tb, out_tm, tb)
    a_rows = _staged_expert_call(x_rows, meta, (w_exp_gate, w_exp_up), (b_exp_gate, b_exp_up), BF16,
                                 _up_compute, "moe_up", tb)
    y_rows = _staged_expert_call(a_rows, meta, (w_exp_down,), (b_exp_down,), F32,
                                 _down_compute, "moe_down", tb)
    return _combine(y_rows, h, route, dest, ln2_g, ln2_b, n_prompt, n_sample, comb_tm)


def _split_w_in(w_in):
    sizes = (GLA_QK, GLA_QK, GLA_WIDTH, GLA_GATE_RANK, GLA_WIDTH, DSWA_WIDTH, DSWA_WIDTH, DSWA_WIDTH)
    o = np.cumsum((0,) + sizes)
    wg = jnp.concatenate([w_in[:, o[0]:o[3]], w_in[:, o[4]:o[5]]], axis=1).astype(BF16)
    wlr = jnp.pad(w_in[:, o[3]:o[4]], ((0, 0), (0, LANES - GLA_GATE_RANK))).astype(BF16)
    wd = w_in[:, o[5]:o[8]].astype(BF16)
    return wg, wlr, wd


def kernel(x_prompt, x_sample, state_gla, cache_dswa_k, cache_dswa_v, w_in, w_gla_alpha, b_gla_alpha,
           gla_norm_w, w_o, ln1_g, ln1_b, w_router, b_router, w_exp_gate, b_exp_gate, w_exp_up,
           b_exp_up, w_exp_down, b_exp_down, ln2_g, ln2_b):
    bp, sp, _ = x_prompt.shape
    bs, ls, _ = x_sample.shape
    n_prompt, n_sample = bp * sp, bs * ls
    n_past = cache_dswa_k.shape[1]

    wg, wlr, wd = _split_w_in(w_in)
    wa = jnp.pad(w_gla_alpha, ((0, LANES - GLA_GATE_RANK), (0, 0))).astype(BF16)
    ba = b_gla_alpha.reshape(1, GLA_QK)
    nw = gla_norm_w.reshape(1, GLA_WIDTH)
    xp = x_prompt.reshape(n_prompt, D_MODEL)
    xs = x_sample.reshape(n_sample, D_MODEL)

    q, k, v, r, g = _proj_gla(xp, wg, wlr, wa, ba, PROJ_TM)
    og_p, state_p = _gla(q, k, g, v, r, nw, jnp.zeros((bp, GLA_HEADS, GLA_DK, GLA_DV), F32),
                         bp, sp, GLA_ROWS, GLA_STEP_HEADS)
    dqb, dk, dv, dkb, dvb = _proj_dswa(xp, wd, PROJ_TM)
    od_p = _dswa_prompt(dqb, dkb, dvb, bp, sp, ATT_T, ATT_HEADS)

    qs, ks, vs, rs, gs = _proj_gla(xs, wg, wlr, wa, ba, n_sample)
    lpad = GLA_CHUNK - ls
    pad_rows = lambda a: jnp.pad(a.reshape(bs, ls, -1), ((0, 0), (0, lpad), (0, 0))).reshape(bs * GLA_CHUNK, -1)
    og_s_pad, state_s = _gla(pad_rows(qs), pad_rows(ks), pad_rows(gs), pad_rows(vs), pad_rows(rs), nw,
                             state_gla, bs, GLA_CHUNK, GLA_CHUNK, GLA_STEP_HEADS)
    og_s = og_s_pad.reshape(bs, GLA_CHUNK, GLA_WIDTH)[:, :ls].reshape(n_sample, GLA_WIDTH)
    tqb, tk, tv, _, _ = _proj_dswa(xs, wd, n_sample)
    heads4 = lambda a: a.reshape(bs, ls, DSWA_HEADS, DSWA_HEAD_DIM)
    od_s = _dswa_sample(heads4(tqb), heads4(tk), heads4(tv), cache_dswa_k, cache_dswa_v)
    od_s = od_s.reshape(n_sample, DSWA_WIDTH).astype(BF16)

    spad = ((0, OUT_TM - n_sample), (0, 0))
    wr = jnp.pad(w_router, ((0, 0), (0, LANES - N_EXPERTS))).astype(BF16)
    br = jnp.pad(b_router, (0, LANES - N_EXPERTS), constant_values=NEG_BIG).reshape(1, LANES)
    h, route, cnt = _outproj(og_p, od_p, xp, jnp.pad(og_s, spad), jnp.pad(od_s, spad), jnp.pad(xs, spad),
                             w_o.astype(BF16), ln1_g.reshape(1, D_MODEL), ln1_b.reshape(1, D_MODEL),
                             wr, br, OUT_TM)
    y_p, y_s = _moe_and_norm(h, route, cnt, n_prompt, n_sample, w_exp_gate, b_exp_gate, w_exp_up,
                             b_exp_up, w_exp_down, b_exp_down, ln2_g.reshape(1, D_MODEL),
                             ln2_b.reshape(1, D_MODEL), MOE_TB, OUT_TM, COMB_TM)

    return (y_p.reshape(bp, sp, D_MODEL), y_s.reshape(bs, ls, D_MODEL), state_p,
            dk.reshape(bp, sp, DSWA_HEADS, DSWA_HEAD_DIM), dv.reshape(bp, sp, DSWA_HEADS, DSWA_HEAD_DIM),
            state_s, tk.reshape(bs, ls, DSWA_HEADS, DSWA_HEAD_DIM),
            tv.reshape(bs, ls, DSWA_HEADS, DSWA_HEAD_DIM))
```

```python
import functools

import numpy as np
import jax
import jax.numpy as jnp
from jax import lax
from jax.experimental import pallas as pl
from jax.experimental.pallas import tpu as pltpu

F32 = jnp.float32
BF16 = jnp.bfloat16
I32 = jnp.int32

D_MODEL = 2048
GLA_HEADS = 4
GLA_DK = 128
GLA_DV = 256
GLA_WIDTH = GLA_HEADS * GLA_DV
GLA_QK = GLA_HEADS * GLA_DK
GLA_GATE_RANK = 16
GLA_TAU = 16.0
GLA_CHUNK = 64
DSWA_HEADS = 8
DSWA_HEAD_DIM = 128
DSWA_WIDTH = DSWA_HEADS * DSWA_HEAD_DIM
DSWA_BRANCHES = ((128, 1), (512, 4), (2048, 16))
N_EXPERTS = 32
TOP_K = 4
D_FF = D_MODEL
SWIGLU_LIMIT = 7.0
SWIGLU_ALPHA = 1.702
DEEPNORM_ALPHA = 2.0 ** 0.25
LN_EPS = 1e-5
RMS_EPS = 1e-6
NEG_BIG = -1e30

LANES = 128
V7X_VMEM_LIMIT = 56 * 1024 * 1024

PROJ_TM = 512
GLA_ROWS = 256
GLA_STEP_HEADS = 2
ATT_T = 256
ATT_HEADS = 8
OUT_TM = 256
MOE_TB = 256
MOE_W_CHUNK = 128
COMB_TM = 128


def _dot(a, b):
    return jnp.dot(a, b, preferred_element_type=F32)


def _dot_nt(a, b):
    return lax.dot_general(a, b, (((1,), (1,)), ((), ())), preferred_element_type=F32)


def _params(sem, vmem=V7X_VMEM_LIMIT):
    return pltpu.CompilerParams(dimension_semantics=sem, vmem_limit_bytes=vmem)


def _proj_gla_kernel(x_ref, wg_ref, wlr_ref, wa_ref, ba_ref, q_ref, k_ref, v_ref, r_ref, g_ref):
    xb = x_ref[...].astype(BF16)
    q_ref[...] = _dot(xb, wg_ref[:, 0:GLA_QK])
    k_ref[...] = _dot(xb, wg_ref[:, GLA_QK:2 * GLA_QK])
    v_ref[...] = _dot(xb, wg_ref[:, 2 * GLA_QK:2 * GLA_QK + GLA_WIDTH]).astype(v_ref.dtype)
    r_ref[...] = _dot(xb, wg_ref[:, 2 * GLA_QK + GLA_WIDTH:])
    lr = _dot(xb, wlr_ref[...])
    z = _dot(lr.astype(BF16), wa_ref[...]) + ba_ref[...]
    log_sig = jnp.minimum(z, 0.0) - jnp.log1p(jnp.exp(-jnp.abs(z)))
    g_ref[...] = log_sig * (1.0 / GLA_TAU)


def _proj_gla(x, wg, wlr, wa, ba, tm):
    m = x.shape[0]
    const = lambda i: (0, 0)
    row = lambda i: (i, 0)
    return pl.pallas_call(
        _proj_gla_kernel,
        grid=(m // tm,),
        in_specs=[pl.BlockSpec((tm, D_MODEL), row),
                  pl.BlockSpec(wg.shape, const, pipeline_mode=pl.Buffered(1)),
                  pl.BlockSpec(wlr.shape, const),
                  pl.BlockSpec(wa.shape, const),
                  pl.BlockSpec(ba.shape, const)],
        out_specs=[pl.BlockSpec((tm, GLA_QK), row), pl.BlockSpec((tm, GLA_QK), row),
                   pl.BlockSpec((tm, GLA_WIDTH), row), pl.BlockSpec((tm, GLA_WIDTH), row),
                   pl.BlockSpec((tm, GLA_QK), row)],
        out_shape=[jax.ShapeDtypeStruct((m, GLA_QK), F32), jax.ShapeDtypeStruct((m, GLA_QK), F32),
                   jax.ShapeDtypeStruct((m, GLA_WIDTH), BF16), jax.ShapeDtypeStruct((m, GLA_WIDTH), F32),
                   jax.ShapeDtypeStruct((m, GLA_QK), F32)],
        compiler_params=_params(("parallel",)),
        name="proj_gla",
    )(x, wg, wlr, wa, ba)


def _proj_dswa_kernel(x_ref, w_ref, qb_ref, k_ref, v_ref, kb_ref, vb_ref):
    xb = x_ref[...].astype(BF16)
    w = DSWA_WIDTH
    qb_ref[...] = (_dot(xb, w_ref[:, 0:w]) * (DSWA_HEAD_DIM ** -0.5)).astype(BF16)
    k = _dot(xb, w_ref[:, w:2 * w])
    k_ref[...] = k
    kb_ref[...] = k.astype(BF16)
    v = _dot(xb, w_ref[:, 2 * w:3 * w])
    v_ref[...] = v
    vb_ref[...] = v.astype(BF16)


def _proj_dswa(x, w, tm):
    m = x.shape[0]
    row = lambda i: (i, 0)
    ospec = pl.BlockSpec((tm, DSWA_WIDTH), row)
    return pl.pallas_call(
        _proj_dswa_kernel,
        grid=(m // tm,),
        in_specs=[pl.BlockSpec((tm, D_MODEL), row),
                  pl.BlockSpec(w.shape, lambda i: (0, 0), pipeline_mode=pl.Buffered(1))],
        out_specs=[ospec] * 5,
        out_shape=[jax.ShapeDtypeStruct((m, DSWA_WIDTH), BF16),
                   jax.ShapeDtypeStruct((m, DSWA_WIDTH), F32),
                   jax.ShapeDtypeStruct((m, DSWA_WIDTH), F32),
                   jax.ShapeDtypeStruct((m, DSWA_WIDTH), BF16),
                   jax.ShapeDtypeStruct((m, DSWA_WIDTH), BF16)],
        compiler_params=_params(("parallel",)),
        name="proj_dswa",
    )(x, w)


def _gla_constants():
    c = GLA_CHUNK
    t = np.arange(c)
    tril = (t[None, :] <= t[:, None]).astype(np.float32)
    rows = [t, np.full(c, 31), 32 * (t // 32) + 15, 16 * (t // 16) + 7, np.full(c, c - 1)]
    cum = np.concatenate([tril[r] for r in rows], axis=0)

    def level_mask(half):
        blk = 2 * half
        return ((t[:, None] // blk == t[None, :] // blk) & (t[:, None] % blk >= half)
                & (t[None, :] % blk < half)).astype(np.float32)

    lmask = np.stack([level_mask(32), level_mask(16), level_mask(8)])
    dmask = np.stack([((t[None, :] == 8 * (t[:, None] // 8) + s) & (t[:, None] % 8 >= s)).astype(np.float32)
                      for s in range(8)])
    return jnp.asarray(cum, BF16), jnp.asarray(lmask), jnp.asarray(dmask)


def _gla_kernel(q_ref, k_ref, g_ref, v_ref, r_ref, nw_ref, s0_ref, cum_ref, lmask_ref, dmask_ref,
                o_ref, sout_ref, s_sc, *, n_chunks, n_steps, heads):
    step = pl.program_id(2)
    c = GLA_CHUNK

    @pl.when(step == 0)
    def _():
        s_sc[...] = s0_ref[0]

    for ci, hh in [(ci, hh) for ci in range(n_chunks) for hh in range(heads)]:
        rows = slice(ci * c, (ci + 1) * c)
        kcols = slice(hh * GLA_DK, (hh + 1) * GLA_DK)
        vcols = slice(hh * GLA_DV, (hh + 1) * GLA_DV)
        q = q_ref[rows, kcols] * (GLA_DK ** -0.5)
        k = k_ref[rows, kcols]
        g = g_ref[rows, kcols]
        v = v_ref[rows, vcols]
        g1 = g.astype(BF16)
        e1 = g - g1.astype(F32)
        g2 = e1.astype(BF16)
        g3 = (e1 - g2.astype(F32)).astype(BF16)
        cb = _dot(cum_ref[...], jnp.concatenate([g1, g2, g3], axis=1))
        cb = cb[:, 0:GLA_DK] + cb[:, GLA_DK:2 * GLA_DK] + cb[:, 2 * GLA_DK:3 * GLA_DK]
        b = cb[0:c]
        b_end = cb[4 * c:5 * c]
        state = s_sc[hh]
        o = _dot((q * jnp.exp(b)).astype(BF16), state.astype(BF16))
        att = jnp.zeros((c, c), F32)
        for li in range(3):
            bl = cb[(li + 1) * c:(li + 2) * c]
            ql = (q * jnp.exp(jnp.minimum(b - bl, 0.0))).astype(BF16)
            kl = (k * jnp.exp(jnp.minimum(bl - b, 0.0))).astype(BF16)
            att = att + _dot_nt(ql, kl) * lmask_ref[li]
        k3 = k.reshape(c // 8, 8, GLA_DK)
        b3 = b.reshape(c // 8, 8, GLA_DK)
        for sl in range(8):
            kb = jnp.broadcast_to(k3[:, sl:sl + 1, :], k3.shape).reshape(c, GLA_DK)
            bb = jnp.broadcast_to(b3[:, sl:sl + 1, :], b3.shape).reshape(c, GLA_DK)
            d = q * kb * jnp.exp(jnp.minimum(b - bb, 0.0))
            att = att + jnp.sum(d, axis=-1, keepdims=True) * dmask_ref[sl]
        o = o + _dot(att.astype(BF16), v)
        kd_t = (k * jnp.exp(b_end - b)).T.astype(BF16)
        e_col = jnp.exp(b_end).T[:, 0:1]
        s_sc[hh] = state * e_col + _dot(kd_t, v)
        on = o * lax.rsqrt(jnp.mean(o * o, axis=-1, keepdims=True) + RMS_EPS) * nw_ref[:, vcols]
        r = r_ref[rows, vcols]
        o_ref[rows, vcols] = (on * (r * jax.nn.sigmoid(r))).astype(o_ref.dtype)

    @pl.when(step == n_steps - 1)
    def _():
        sout_ref[0] = s_sc[...]


def _gla(q, k, g, v, r, nw, s0, batch, length, rows_per_step, heads):
    n_steps = length // rows_per_step
    cum, lmask, dmask = _gla_constants()
    qk_spec = pl.BlockSpec((rows_per_step, heads * GLA_DK), lambda b, h, t: (b * n_steps + t, h))
    v_spec = pl.BlockSpec((rows_per_step, heads * GLA_DV), lambda b, h, t: (b * n_steps + t, h))
    s_spec = pl.BlockSpec((1, heads, GLA_DK, GLA_DV), lambda b, h, t: (b, h, 0, 0))
    const2 = lambda b, h, t: (0, 0)
    const3 = lambda b, h, t: (0, 0, 0)
    kern = functools.partial(_gla_kernel, n_chunks=rows_per_step // GLA_CHUNK, n_steps=n_steps, heads=heads)
    return pl.pallas_call(
        kern,
        grid=(batch, GLA_HEADS // heads, n_steps),
        in_specs=[qk_spec, qk_spec, qk_spec, v_spec, v_spec,
                  pl.BlockSpec((1, heads * GLA_DV), lambda b, h, t: (0, h)),
                  s_spec,
                  pl.BlockSpec(cum.shape, const2),
                  pl.BlockSpec(lmask.shape, const3),
                  pl.BlockSpec(dmask.shape, const3)],
        out_specs=[v_spec, s_spec],
        out_shape=[jax.ShapeDtypeStruct((batch * length, GLA_WIDTH), BF16),
                   jax.ShapeDtypeStruct((batch, GLA_HEADS, GLA_DK, GLA_DV), F32)],
        scratch_shapes=[pltpu.VMEM((heads, GLA_DK, GLA_DV), F32)],
        compiler_params=_params(("parallel", "parallel", "arbitrary")),
        name="gla",
    )(q, k, g, v, r, nw, s0, cum, lmask, dmask)


def _alibi_slopes():
    return 2.0 ** (-8.0 * (np.arange(DSWA_HEADS, dtype=np.float64) + 1.0) / DSWA_HEADS)


def _branch_bias(dist):
    dist = np.asarray(dist, np.int64)
    mult = np.zeros(dist.shape, np.float64)
    for window, dil in DSWA_BRANCHES:
        mult += (dist >= 0) & (dist <= window) & (dist % dil == 0)
    slopes = _alibi_slopes().reshape((-1,) + (1,) * dist.ndim)
    bias = np.where(mult > 0, np.log(np.maximum(mult, 1.0)) - slopes * dist, NEG_BIG)
    return bias.astype(np.float32)


def _dswa_kernel(q_ref, k_ref, v_ref, bias_ref, o_ref, m_sc, l_sc, acc_sc, *, blk, heads):
    qi = pl.program_id(2)
    e = DSWA_HEAD_DIM
    m_sc[...] = jnp.full(m_sc.shape, NEG_BIG, F32)
    l_sc[...] = jnp.zeros(l_sc.shape, F32)
    acc_sc[...] = jnp.zeros(acc_sc.shape, F32)

    def body(kj, carry):
        off = pl.multiple_of(kj * blk, blk)
        for g in range(heads):
            cols = slice(g * e, (g + 1) * e)
            kb = k_ref[0, pl.ds(off, blk), cols]
            vb = v_ref[0, pl.ds(off, blk), cols]
            s = _dot_nt(q_ref[:, cols], kb) + bias_ref[g, qi - kj]
            m_prev = m_sc[g]
            m_new = jnp.maximum(m_prev, jnp.max(s, axis=-1, keepdims=True))
            p = jnp.exp(s - jnp.concatenate([m_new] * (blk // LANES), axis=1))
            alpha = jnp.exp(m_prev - m_new)
            l_sc[g] = alpha * l_sc[g] + jnp.sum(p, axis=-1, keepdims=True)
            acc_sc[g] = alpha * acc_sc[g] + _dot(p.astype(BF16), vb)
            m_sc[g] = m_new
        return carry

    lax.fori_loop(0, qi + 1, body, 0)
    for g in range(heads):
        o_ref[:, g * e:(g + 1) * e] = (acc_sc[g] / l_sc[g]).astype(o_ref.dtype)


def _dswa_prompt(qb, kb, vb, batch, seq, blk, heads):
    nq = seq // blk
    t = np.arange(blk)
    dist = (np.arange(nq)[:, None, None] * blk + t[None, :, None] - t[None, None, :])
    bias = jnp.asarray(_branch_bias(dist))
    k3 = kb.reshape(batch, seq, DSWA_WIDTH)
    v3 = vb.reshape(batch, seq, DSWA_WIDTH)
    width = heads * DSWA_HEAD_DIM
    q_spec = pl.BlockSpec((blk, width), lambda h, b, i: (b * nq + i, h))
    kv_spec = pl.BlockSpec((1, seq, width), lambda h, b, i: (b, 0, h))
    stat = pltpu.VMEM((heads, blk, DSWA_HEAD_DIM), F32)
    return pl.pallas_call(
        functools.partial(_dswa_kernel, blk=blk, heads=heads),
        grid=(DSWA_HEADS // heads, batch, nq),
        in_specs=[q_spec, kv_spec, kv_spec,
                  pl.BlockSpec((heads, nq, blk, blk), lambda h, b, i: (h, 0, 0, 0),
                               pipeline_mode=pl.Buffered(1))],
        out_specs=q_spec,
        out_shape=jax.ShapeDtypeStruct((batch * seq, DSWA_WIDTH), BF16),
        scratch_shapes=[stat, stat, stat],
        compiler_params=_params(("parallel", "parallel", "arbitrary")),
        name="dswa_prompt",
    )(qb, k3, v3, bias)


def _dswa_sample_kernel(q_ref, ka_ref, va_ref, kb_ref, vb_ref, kn_ref, vn_ref, ba_ref, bb_ref, bn_ref,
                        o_ref):
    e = DSWA_HEAD_DIM
    q = q_ref[0]
    flat = lambda ref_val: ref_val.reshape(-1, e).astype(BF16)
    sa = _dot_nt(q, flat(ka_ref[0, 0])) + ba_ref[...]
    sb = _dot_nt(q, flat(kb_ref[0])) + bb_ref[...]
    sn = _dot_nt(q, kn_ref[0]) + bn_ref[...]
    rowmax = lambda s: jnp.max(s, axis=-1, keepdims=True)
    m = jnp.maximum(jnp.maximum(rowmax(sa), rowmax(sb)), rowmax(sn))
    pa = jnp.exp(sa - m)
    pb = jnp.exp(sb - m)
    pn = jnp.exp(sn - m)
    rowsum = lambda p: jnp.sum(p, axis=-1, keepdims=True)
    den = rowsum(pa) + rowsum(pb) + rowsum(pn)
    out = (_dot(pa.astype(BF16), flat(va_ref[0, 0])) + _dot(pb.astype(BF16), flat(vb_ref[0]))
           + _dot(pn.astype(BF16), vn_ref[0]))
    o_ref[0] = out / den


def _cross_head_bias(bias):
    h, n_new, n_keys = bias.shape
    full = np.full((h, n_new, n_keys, h), NEG_BIG, np.float32)
    for hh in range(h):
        full[hh, :, :, hh] = bias[hh]
    return full.reshape(h * n_new, n_keys * h)


def _dswa_sample(qb, k_new, v_new, cache_k, cache_v):
    batch, n_new, h, e = qb.shape
    n_past = cache_k.shape[1]
    tail = DSWA_BRANCHES[1][0]
    wide_window, wide_dil = DSWA_BRANCHES[2]
    assert DSWA_BRANCHES[0][0] <= tail
    assert n_past % tail == 0 and n_past % wide_dil == 0 and tail % wide_dil == 0 and n_new <= wide_dil
    n_groups = (n_past - tail) // wide_dil
    rows = h * n_new
    lq = np.arange(n_new)
    pos_a = n_past - tail + np.arange(tail)
    pos_b = (wide_dil * np.arange(n_groups)[:, None] + np.arange(n_new)[None, :]).reshape(-1)
    bias_a = _cross_head_bias(_branch_bias(n_past + lq[:, None] - pos_a[None, :]))
    bias_b = _cross_head_bias(_branch_bias(n_past + lq[:, None] - pos_b[None, :]))
    bias_n = np.full((rows, LANES), NEG_BIG, np.float32)
    bias_n[:, :n_new * h] = _cross_head_bias(_branch_bias(lq[:, None] - lq[None, :]))
    q2 = qb.transpose(0, 2, 1, 3).reshape(batch, rows, e)
    pad = ((0, 0), (0, LANES - n_new * h), (0, 0))
    kn = jnp.pad(k_new.reshape(batch, n_new * h, e).astype(BF16), pad)
    vn = jnp.pad(v_new.reshape(batch, n_new * h, e).astype(BF16), pad)
    tail_view = lambda c: c.reshape(batch, n_past // tail, tail, h, e)
    wide_view = lambda c: c.reshape(batch, n_past // wide_dil, wide_dil, h, e)
    tail_spec = pl.BlockSpec((1, 1, tail, h, e), lambda b: (b, n_past // tail - 1, 0, 0, 0))
    wide_spec = pl.BlockSpec((1, n_groups, n_new, h, e), lambda b: (b, 0, 0, 0, 0))
    b3 = lambda b: (b, 0, 0)
    c2 = lambda b: (0, 0)
    out = pl.pallas_call(
        _dswa_sample_kernel,
        grid=(batch,),
        in_specs=[pl.BlockSpec((1, rows, e), b3), tail_spec, tail_spec, wide_spec, wide_spec,
                  pl.BlockSpec((1, LANES, e), b3), pl.BlockSpec((1, LANES, e), b3),
                  pl.BlockSpec(bias_a.shape, c2), pl.BlockSpec(bias_b.shape, c2),
                  pl.BlockSpec(bias_n.shape, c2)],
        out_specs=pl.BlockSpec((1, rows, e), b3),
        out_shape=jax.ShapeDtypeStruct((batch, rows, e), F32),
        compiler_params=_params(("parallel",)),
        name="dswa_sample",
    )(q2, tail_view(cache_k), tail_view(cache_v), wide_view(cache_k), wide_view(cache_v), kn, vn,
      jnp.asarray(bias_a), jnp.asarray(bias_b), jnp.asarray(bias_n))
    return out.reshape(batch, h, n_new, e).transpose(0, 2, 1, 3).reshape(batch, n_new, h * e)


def _layer_norm(pre, g, b):
    mu = jnp.mean(pre, axis=-1, keepdims=True)
    cen = pre - mu
    var = jnp.mean(cen * cen, axis=-1, keepdims=True)
    return cen * lax.rsqrt(var + LN_EPS) * g + b


def _outproj_kernel(ogp_ref, odp_ref, xp_ref, ogs_ref, ods_ref, xs_ref, wo_ref, g_ref, b_ref,
                    wr_ref, br_ref, tri_ref, h_ref, route_ref, cnt_ref, cnt_sc, *, n_prompt_tiles):
    i = pl.program_id(0)
    is_s = i >= n_prompt_tiles

    @pl.when(i == 0)
    def _():
        cnt_sc[...] = jnp.zeros(cnt_sc.shape, F32)

    og = jnp.where(is_s, ogs_ref[...], ogp_ref[...])
    od = jnp.where(is_s, ods_ref[...], odp_ref[...])
    x = jnp.where(is_s, xs_ref[...], xp_ref[...])
    acc = _dot(og, wo_ref[0:GLA_WIDTH, :]) + _dot(od, wo_ref[GLA_WIDTH:, :])
    h = _layer_norm(DEEPNORM_ALPHA * x + acc, g_ref[...], b_ref[...])
    h_ref[...] = h
    logits = _dot(h.astype(BF16), wr_ref[...]) + br_ref[...]
    lane = lax.broadcasted_iota(I32, logits.shape, 1).astype(F32)
    tops, sels, idxs = [], [], []
    cur = logits
    for _ in range(TOP_K):
        mx = jnp.max(cur, axis=-1, keepdims=True)
        idx = jnp.min(jnp.where(cur == mx, lane, float(LANES)), axis=-1, keepdims=True)
        sel = lane == idx
        tops.append(mx)
        sels.append(sel)
        idxs.append(idx)
        cur = jnp.where(sel, -jnp.inf, cur)
    exps = [jnp.exp(t - tops[0]) for t in tops]
    den = exps[0] + exps[1] + exps[2] + exps[3]
    multi = jnp.zeros(logits.shape, F32)
    for sel in sels:
        multi = multi + jnp.where(sel, 1.0, 0.0)
    rank_all = _dot(tri_ref[...], multi.astype(BF16)) + cnt_sc[...]
    cnt_sc[...] = cnt_sc[...] + jnp.sum(multi, axis=0, keepdims=True)
    cnt_ref[...] = cnt_sc[...]
    route = jnp.zeros(logits.shape, F32)
    for kk in range(TOP_K):
        rank = jnp.sum(jnp.where(sels[kk], rank_all, 0.0), axis=-1, keepdims=True)
        route = jnp.where(lane == kk, idxs[kk], route)
        route = jnp.where(lane == TOP_K + kk, exps[kk] / den, route)
        route = jnp.where(lane == 2 * TOP_K + kk, rank, route)
    route_ref[...] = route


def _outproj(og_p, od_p, x_p, og_s, od_s, x_s, wo, g, b, wr, br, tm):
    n_p = x_p.shape[0] // tm
    tri = jnp.asarray(np.tril(np.ones((tm, tm), np.float32), -1), BF16)
    prow = lambda i: (jnp.minimum(i, n_p - 1), 0)
    c2 = lambda i: (0, 0)
    row = lambda i: (i, 0)
    m_tot = x_p.shape[0] + tm
    return pl.pallas_call(
        functools.partial(_outproj_kernel, n_prompt_tiles=n_p),
        grid=(n_p + 1,),
        in_specs=[pl.BlockSpec((tm, GLA_WIDTH), prow), pl.BlockSpec((tm, DSWA_WIDTH), prow),
                  pl.BlockSpec((tm, D_MODEL), prow),
                  pl.BlockSpec((tm, GLA_WIDTH), c2), pl.BlockSpec((tm, DSWA_WIDTH), c2),
                  pl.BlockSpec((tm, D_MODEL), c2),
                  pl.BlockSpec(wo.shape, c2), pl.BlockSpec(g.shape, c2), pl.BlockSpec(b.shape, c2),
                  pl.BlockSpec(wr.shape, c2), pl.BlockSpec(br.shape, c2), pl.BlockSpec(tri.shape, c2)],
        out_specs=[pl.BlockSpec((tm, D_MODEL), row), pl.BlockSpec((tm, LANES), row),
                   pl.BlockSpec((1, LANES), c2)],
        out_shape=[jax.ShapeDtypeStruct((m_tot, D_MODEL), F32),
                   jax.ShapeDtypeStruct((m_tot, LANES), F32),
                   jax.ShapeDtypeStruct((1, LANES), F32)],
        scratch_shapes=[pltpu.VMEM((1, LANES), F32)],
        compiler_params=_params(("arbitrary",)),
        name="outproj_ln_router",
    )(og_p, od_p, x_p, og_s, od_s, x_s, wo, g, b, wr, br, tri)


def _dispatch_kernel(cnt_ref, pst_ref, dest_ref, h_hbm, xr_ref, hbuf, zero_sc, lsems, sems, zsem, *,
                     tm, tb, n_blocks, n_steps):
    i = pl.program_id(0)
    slot = i % 2
    ring = i % 3

    def tile_load(step):
        rows = pl.ds(pl.multiple_of(step * tm, tm), tm)
        return pltpu.make_async_copy(h_hbm.at[rows, :], hbuf.at[step % 3], lsems.at[step % 3])

    @pl.when(i == 0)
    def _():
        tile_load(0).start()

    @pl.when(i + 1 < n_steps)
    def _():
        tile_load(i + 1).start()

    tile_load(i).wait()
    h_ref = hbuf.at[ring]

    def row_copy(t, r, s):
        return pltpu.make_async_copy(h_ref.at[pl.ds(t, 1), :], xr_ref.at[pl.ds(r, 1), :], sems.at[s])

    def issue(t, carry):
        for kk in range(TOP_K):
            row_copy(t, dest_ref[0, 0, t * TOP_K + kk], slot).start(priority=kk % 2)
        return carry

    lax.fori_loop(0, tm, issue, 0)

    def zero_copy(r):
        return pltpu.make_async_copy(zero_sc.at[pl.ds(0, 1), :], xr_ref.at[pl.ds(r, 1), :], zsem)

    @pl.when(i == 0)
    def _():
        zero_sc[...] = jnp.zeros(zero_sc.shape, F32)

        def per_expert(e, carry):
            cnt = cnt_ref[e]
            base = pst_ref[e]
            end = (cnt + tb - 1) // tb * tb

            def start(r, c2):
                zero_copy(base + r).start()
                return c2

            def wait(r, c2):
                zero_copy(base + r).wait()
                return c2

            lax.fori_loop(cnt, end, start, 0)
            lax.fori_loop(cnt, end, wait, 0)
            return carry

        lax.fori_loop(0, N_EXPERTS, per_expert, 0)

        last = N_EXPERTS - 1
        n_used = (pst_ref[last] + (cnt_ref[last] + tb - 1) // tb * tb) // tb

        def tail_copy(blk):
            return pltpu.make_async_copy(zero_sc, xr_ref.at[pl.ds(pl.multiple_of(blk * tb, tb), tb), :], zsem)

        def tail_start(blk, c2):
            tail_copy(blk).start()
            return c2

        def tail_wait(blk, c2):
            tail_copy(blk).wait()
            return c2

        lax.fori_loop(n_used, n_blocks, tail_start, 0)
        lax.fori_loop(n_used, n_blocks, tail_wait, 0)

    def drain(s):
        def body(t, carry):
            for kk in range(TOP_K):
                row_copy(t, 0, s).wait()
            return carry
        lax.fori_loop(0, tm, body, 0)

    @pl.when(i > 0)
    def _():
        drain(1 - slot)

    @pl.when(i == n_steps - 1)
    def _():
        drain(slot)


def _dispatch(h, dest, counts, pstart, n_rows, tm, tb):
    m = h.shape[0]
    n_t = m // tm
    dest3 = dest.reshape(n_t, 1, tm * TOP_K)
    grid_spec = pltpu.PrefetchScalarGridSpec(
        num_scalar_prefetch=2,
        grid=(n_t,),
        in_specs=[pl.BlockSpec((1, 1, tm * TOP_K), lambda i, c, p: (i, 0, 0), memory_space=pltpu.SMEM),
                  pl.BlockSpec(memory_space=pl.ANY)],
        out_specs=pl.BlockSpec(memory_space=pl.ANY),
        scratch_shapes=[pltpu.VMEM((3, tm, D_MODEL), F32), pltpu.VMEM((tb, D_MODEL), F32),
                        pltpu.SemaphoreType.DMA((3,)), pltpu.SemaphoreType.DMA((2,)),
                        pltpu.SemaphoreType.DMA(())],
    )
    return pl.pallas_call(
        functools.partial(_dispatch_kernel, tm=tm, tb=tb, n_blocks=n_rows // tb, n_steps=n_t),
        grid_spec=grid_spec,
        out_shape=jax.ShapeDtypeStruct((n_rows, D_MODEL), F32),
        compiler_params=_params(("arbitrary",)),
        name="moe_dispatch",
    )(counts, pstart, dest3, h)


def _staged_expert_kernel(be_ref, nvb_ref, lo_ref, hi_ref, kick_ref, nxt_ref, par_ref, x_ref, *rest,
                          n_mats, compute):
    w_refs = rest[:n_mats]
    b_refs = rest[n_mats:2 * n_mats]
    o_ref = rest[2 * n_mats]
    wb, stage, sems = rest[2 * n_mats + 1:]
    i = pl.program_id(0)
    n_chunks = wb.shape[2] // MOE_W_CHUNK

    def chunk_rows(c):
        return pl.ds(pl.multiple_of(c * MOE_W_CHUNK, MOE_W_CHUNK), MOE_W_CHUNK)

    def copies(e, c):
        slot = c % 2
        return [pltpu.make_async_copy(w.at[e, chunk_rows(c), :], stage.at[slot, mi], sems.at[slot])
                for mi, w in enumerate(w_refs)]

    def start(e, c):
        for cp in copies(e, c):
            cp.start()

    def finish_range(e, lo, hi, dst):
        def body(c, carry):
            for cp in copies(e, c):
                cp.wait()
            for mi in range(n_mats):
                wb[dst, mi, chunk_rows(c), :] = stage[c % 2, mi].astype(BF16)

            @pl.when(c + 2 < n_chunks)
            def _():
                start(e, c + 2)

            return carry

        lax.fori_loop(lo, hi, body, 0)

    cur = par_ref[i]

    @pl.when(i == 0)
    def _():
        start(be_ref[0], 0)
        start(be_ref[0], 1)
        finish_range(be_ref[0], 0, n_chunks, cur)

    @pl.when(kick_ref[i] == 1)
    def _():
        start(nxt_ref[i], 0)
        start(nxt_ref[i], 1)

    @pl.when(i < nvb_ref[0])
    def _():
        compute(x_ref, [wb.at[cur, mi] for mi in range(n_mats)], b_refs, o_ref)

    @pl.when(i >= nvb_ref[0])
    def _():
        o_ref[...] = jnp.zeros(o_ref.shape, o_ref.dtype)

    finish_range(nxt_ref[i], lo_ref[i], hi_ref[i], 1 - cur)


def _up_compute(x_ref, w, b_refs, a_ref):
    xb = x_ref[...].astype(BF16)
    half = a_ref.shape[1] // 2
    for n in range(2):
        cols = slice(n * half, (n + 1) * half)
        gate = jnp.minimum(_dot(xb, w[0][:, cols]) + b_refs[0][0, :, cols], SWIGLU_LIMIT)
        up = jnp.clip(_dot(xb, w[1][:, cols]) + b_refs[1][0, :, cols], -SWIGLU_LIMIT, SWIGLU_LIMIT)
        a_ref[:, cols] = (gate * jax.nn.sigmoid(SWIGLU_ALPHA * gate) * (up + 1.0)).astype(a_ref.dtype)


def _down_compute(a_ref, w, b_refs, y_ref):
    ab = a_ref[...]
    half = y_ref.shape[1] // 2
    for n in range(2):
        cols = slice(n * half, (n + 1) * half)
        y_ref[:, cols] = _dot(ab, w[0][:, cols]) + b_refs[0][0, :, cols]


def _staged_expert_call(x_rows, meta, weights, biases, out_dtype, compute, name, tb):
    n_rows, k = x_rows.shape
    n_out = weights[0].shape[2]
    n_mats = len(weights)
    rows_map = lambda i, be, nv, *_: (jnp.minimum(i, nv[0] - 1), 0)
    bias_spec = pl.BlockSpec((1, 1, n_out), lambda i, be, *_: (be[i], 0, 0))
    grid_spec = pltpu.PrefetchScalarGridSpec(
        num_scalar_prefetch=len(meta),
        grid=(n_rows // tb,),
        in_specs=([pl.BlockSpec((tb, k), rows_map)] + [pl.BlockSpec(memory_space=pl.ANY)] * n_mats
                  + [bias_spec] * n_mats),
        out_specs=pl.BlockSpec((tb, n_out), lambda i, *_: (i, 0)),
        scratch_shapes=[pltpu.VMEM((2, n_mats, k, n_out), BF16),
                        pltpu.VMEM((2, n_mats, MOE_W_CHUNK, n_out), F32),
                        pltpu.SemaphoreType.DMA((2,))],
    )
    return pl.pallas_call(
        functools.partial(_staged_expert_kernel, n_mats=n_mats, compute=compute),
        grid_spec=grid_spec,
        out_shape=jax.ShapeDtypeStruct((n_rows, n_out), out_dtype),
        compiler_params=_params(("arbitrary",)),
        name=name,
    )(*meta, x_rows, *weights, *[b.reshape(N_EXPERTS, 1, n_out) for b in biases])


def _expert_schedule(blk_e, nvb, padded, pstart, tb, n_chunks):
    nb = blk_e.shape[0]
    ids = jnp.arange(N_EXPERTS, dtype=I32)
    n_blk = padded // tb
    has = n_blk > 0
    later = jnp.where(has[None, :] & (ids[None, :] > ids[:, None]), ids[None, :], N_EXPERTS)
    nxt_e = jnp.min(later, axis=1)
    nxt_e = jnp.where(nxt_e == N_EXPERTS, -1, nxt_e)
    par_e = jnp.maximum(jnp.cumsum(has.astype(I32)) - 1, 0) % 2
    bi = jnp.arange(nb, dtype=I32)
    rank = bi - (pstart // tb)[blk_e]
    n_e = jnp.maximum(n_blk[blk_e], 1)
    active = (bi < nvb[0]) & (nxt_e[blk_e] >= 0)
    lo = jnp.where(active, n_chunks * rank // n_e, 0)
    hi = jnp.where(active, n_chunks * (rank + 1) // n_e, 0)
    kick = (active & (rank == 0)).astype(I32)
    to_i32 = lambda a: a.astype(I32)
    return tuple(map(to_i32, (blk_e, nvb, lo, hi, kick, nxt_e[blk_e], par_e[blk_e])))


def _combine_kernel(dcur_ref, dnxt_ref, y_ref, h_ref, gate_ref, g_ref, b_ref, yp_ref, ys_ref,
                    buf, sems, *, tm, n_tiles, n_prompt_tiles):
    i = pl.program_id(0)
    slot = i % 2

    def row_copy(dref, t, kk, sl):
        r = dref[0, 0, t * TOP_K + kk]
        return pltpu.make_async_copy(y_ref.at[pl.ds(r, 1), :], buf.at[sl, kk, pl.ds(t, 1), :], sems.at[sl])

    def issue(dref, sl):
        def body(t, carry):
            for kk in range(TOP_K):
                row_copy(dref, t, kk, sl).start(priority=kk % 2)
            return carry
        lax.fori_loop(0, tm, body, 0)

    @pl.when(i == 0)
    def _():
        issue(dcur_ref, 0)

    @pl.when(i + 1 < n_tiles)
    def _():
        issue(dnxt_ref, 1 - slot)

    def drain(t, carry):
        for kk in range(TOP_K):
            row_copy(dcur_ref, t, kk, slot).wait()
        return carry

    lax.fori_loop(0, tm, drain, 0)
    gate = gate_ref[...]
    f = jnp.zeros((tm, D_MODEL), F32)
    for kk in range(TOP_K):
        f = f + buf[slot, kk] * gate[:, TOP_K + kk:TOP_K + kk + 1]
    y = _layer_norm(DEEPNORM_ALPHA * h_ref[...] + f, g_ref[...], b_ref[...])

    @pl.when(i < n_prompt_tiles)
    def _():
        yp_ref[...] = y

    @pl.when(i >= n_prompt_tiles)
    def _():
        ys_ref[...] = y


def _combine(y_rows, h, route, dest, g, b, n_prompt, n_sample, tm):
    n_p = n_prompt // tm
    n_tiles = n_p + n_sample // tm
    dest3 = dest.reshape(-1, 1, tm * TOP_K)
    smem = functools.partial(pl.BlockSpec, (1, 1, tm * TOP_K), memory_space=pltpu.SMEM)
    c2 = lambda i: (0, 0)
    return pl.pallas_call(
        functools.partial(_combine_kernel, tm=tm, n_tiles=n_tiles, n_prompt_tiles=n_p),
        grid=(n_tiles,),
        in_specs=[smem(lambda i: (i, 0, 0)),
                  smem(lambda i: (jnp.minimum(i + 1, n_tiles - 1), 0, 0)),
                  pl.BlockSpec(memory_space=pl.ANY),
                  pl.BlockSpec((tm, D_MODEL), lambda i: (i, 0)),
                  pl.BlockSpec((tm, LANES), lambda i: (i, 0)),
                  pl.BlockSpec(g.shape, c2), pl.BlockSpec(b.shape, c2)],
        out_specs=[pl.BlockSpec((tm, D_MODEL), lambda i: (jnp.minimum(i, n_p - 1), 0)),
                   pl.BlockSpec((tm, D_MODEL), lambda i: (jnp.maximum(i - n_p, 0), 0))],
        out_shape=[jax.ShapeDtypeStruct((n_prompt, D_MODEL), F32),
                   jax.ShapeDtypeStruct((n_sample, D_MODEL), F32)],
        scratch_shapes=[pltpu.VMEM((2, TOP_K, tm, D_MODEL), F32), pltpu.SemaphoreType.DMA((2,))],
        compiler_params=_params(("arbitrary",)),
        name="moe_combine_ln",
    )(dest3, dest3, y_rows, h, route, g, b)


def _moe_and_norm(h, route, cnt, n_prompt, n_sample, w_exp_gate, b_exp_gate, w_exp_up, b_exp_up,
                  w_exp_down, b_exp_down, ln2_g, ln2_b, tb, out_tm, comb_tm):
    m_tot = h.shape[0]
    top_e = route[:, 0:TOP_K].astype(I32)
    rank = route[:, 2 * TOP_K:3 * TOP_K].astype(I32)
    counts = cnt[0, 0:N_EXPERTS].astype(I32)
    padded = (counts + tb - 1) // tb * tb
    pend = jnp.cumsum(padded)
    pstart = pend - padded
    dest = (pstart[top_e] + rank).reshape(-1)
    nb = (m_tot * TOP_K + N_EXPERTS * (tb - 1) + tb - 1) // tb
    nvb = (pend[-1:] // tb).astype(I32)
    blk_start = jnp.arange(nb, dtype=I32) * tb
    blk_e = jnp.minimum(jnp.sum((blk_start[:, None] >= pend[None, :]).astype(I32), axis=1), N_EXPERTS - 1)
    meta = _expert_schedule(blk_e, nvb, padded, pstart, tb, D_MODEL // MOE_W_CHUNK)
    x_rows = _dispatch(h, dest, counts, pstart.astype(I32), nb * tb, out_tm, tb)
    a_rows = _staged_expert_call(x_rows, meta, (w_exp_gate, w_exp_up), (b_exp_gate, b_exp_up), BF16,
                                 _up_compute, "moe_up", tb)
    y_rows = _staged_expert_call(a_rows, meta, (w_exp_down,), (b_exp_down,), F32,
                                 _down_compute, "moe_down", tb)
    return _combine(y_rows, h, route, dest, ln2_g, ln2_b, n_prompt, n_sample, comb_tm)


def _split_w_in(w_in):
    sizes = (GLA_QK, GLA_QK, GLA_WIDTH, GLA_GATE_RANK, GLA_WIDTH, DSWA_WIDTH, DSWA_WIDTH, DSWA_WIDTH)
    o = np.cumsum((0,) + sizes)
    wg = jnp.concatenate([w_in[:, o[0]:o[3]], w_in[:, o[4]:o[5]]], axis=1).astype(BF16)
    wlr = jnp.pad(w_in[:, o[3]:o[4]], ((0, 0), (0, LANES - GLA_GATE_RANK))).astype(BF16)
    wd = w_in[:, o[5]:o[8]].astype(BF16)
    return wg, wlr, wd


def kernel(x_prompt, x_sample, state_gla, cache_dswa_k, cache_dswa_v, w_in, w_gla_alpha, b_gla_alpha,
           gla_norm_w, w_o, ln1_g, ln1_b, w_router, b_router, w_exp_gate, b_exp_gate, w_exp_up,
           b_exp_up, w_exp_down, b_exp_down, ln2_g, ln2_b):
    bp, sp, _ = x_prompt.shape
    bs, ls, _ = x_sample.shape
    n_prompt, n_sample = bp * sp, bs * ls
    n_past = cache_dswa_k.shape[1]

    wg, wlr, wd = _split_w_in(w_in)
    wa = jnp.pad(w_gla_alpha, ((0, LANES - GLA_GATE_RANK), (0, 0))).astype(BF16)
    ba = b_gla_alpha.reshape(1, GLA_QK)
    nw = gla_norm_w.reshape(1, GLA_WIDTH)
    xp = x_prompt.reshape(n_prompt, D_MODEL)
    xs = x_sample.reshape(n_sample, D_MODEL)

    q, k, v, r, g = _proj_gla(xp, wg, wlr, wa, ba, PROJ_TM)
    og_p, state_p = _gla(q, k, g, v, r, nw, jnp.zeros((bp, GLA_HEADS, GLA_DK, GLA_DV), F32),
                         bp, sp, GLA_ROWS, GLA_STEP_HEADS)
    dqb, dk, dv, dkb, dvb = _proj_dswa(xp, wd, PROJ_TM)
    od_p = _dswa_prompt(dqb, dkb, dvb, bp, sp, ATT_T, ATT_HEADS)

    qs, ks, vs, rs, gs = _proj_gla(xs, wg, wlr, wa, ba, n_sample)
    lpad = GLA_CHUNK - ls
    pad_rows = lambda a: jnp.pad(a.reshape(bs, ls, -1), ((0, 0), (0, lpad), (0, 0))).reshape(bs * GLA_CHUNK, -1)
    og_s_pad, state_s = _gla(pad_rows(qs), pad_rows(ks), pad_rows(gs), pad_rows(vs), pad_rows(rs), nw,
                             state_gla, bs, GLA_CHUNK, GLA_CHUNK, GLA_STEP_HEADS)
    og_s = og_s_pad.reshape(bs, GLA_CHUNK, GLA_WIDTH)[:, :ls].reshape(n_sample, GLA_WIDTH)
    tqb, tk, tv, _, _ = _proj_dswa(xs, wd, n_sample)
    heads4 = lambda a: a.reshape(bs, ls, DSWA_HEADS, DSWA_HEAD_DIM)
    od_s = _dswa_sample(heads4(tqb), heads4(tk), heads4(tv), cache_dswa_k, cache_dswa_v)
    od_s = od_s.reshape(n_sample, DSWA_WIDTH).astype(BF16)

    spad = ((0, OUT_TM - n_sample), (0, 0))
    wr = jnp.pad(w_router, ((0, 0), (0, LANES - N_EXPERTS))).astype(BF16)
    br = jnp.pad(b_router, (0, LANES - N_EXPERTS), constant_values=NEG_BIG).reshape(1, LANES)
    h, route, cnt = _outproj(og_p, od_p, xp, jnp.pad(og_s, spad), jnp.pad(od_s, spad), jnp.pad(xs, spad),
                             w_o.astype(BF16), ln1_g.reshape(1, D_MODEL), ln1_b.reshape(1, D_MODEL),
                             wr, br, OUT_TM)
    y_p, y_s = _moe_and_norm(h, route, cnt, n_prompt, n_sample, w_exp_gate, b_exp_gate, w_exp_up,
                             b_exp_up, w_exp_down, b_exp_down, ln2_g.reshape(1, D_MODEL),
                             ln2_b.reshape(1, D_MODEL), MOE_TB, OUT_TM, COMB_TM)

    return (y_p.reshape(bp, sp, D_MODEL), y_s.reshape(bs, ls, D_MODEL), state_p,
            dk.reshape(bp, sp, DSWA_HEADS, DSWA_HEAD_DIM), dv.reshape(bp, sp, DSWA_HEADS, DSWA_HEAD_DIM),
            state_s, tk.reshape(bs, ls, DSWA_HEADS, DSWA_HEAD_DIM),
            tv.reshape(bs, ls, DSWA_HEADS, DSWA_HEAD_DIM))
```

```python
import functools

import numpy as np
import jax
import jax.numpy as jnp
from jax import lax
from jax.experimental import pallas as pl
from jax.experimental.pallas import tpu as pltpu

F32 = jnp.float32
BF16 = jnp.bfloat16
I32 = jnp.int32

D_MODEL = 2048
GLA_HEADS = 4
GLA_DK = 128
GLA_DV = 256
GLA_WIDTH = GLA_HEADS * GLA_DV
GLA_QK = GLA_HEADS * GLA_DK
GLA_GATE_RANK = 16
GLA_TAU = 16.0
GLA_CHUNK = 64
DSWA_HEADS = 8
DSWA_HEAD_DIM = 128
DSWA_WIDTH = DSWA_HEADS * DSWA_HEAD_DIM
DSWA_BRANCHES = ((128, 1), (512, 4), (2048, 16))
N_EXPERTS = 32
TOP_K = 4
D_FF = D_MODEL
SWIGLU_LIMIT = 7.0
SWIGLU_ALPHA = 1.702
DEEPNORM_ALPHA = 2.0 ** 0.25
LN_EPS = 1e-5
RMS_EPS = 1e-6
NEG_BIG = -1e30

LANES = 128
V7X_VMEM_LIMIT = 56 * 1024 * 1024

PROJ_TM = 512
GLA_ROWS = 256
GLA_STEP_HEADS = 2
ATT_T = 256
ATT_HEADS = 8
OUT_TM = 256
MOE_TB = 256
MOE_W_CHUNK = 128
COMB_TM = 128


def _dot(a, b):
    return jnp.dot(a, b, preferred_element_type=F32)


def _dot_nt(a, b):
    return lax.dot_general(a, b, (((1,), (1,)), ((), ())), preferred_element_type=F32)


def _params(sem, vmem=V7X_VMEM_LIMIT):
    return pltpu.CompilerParams(dimension_semantics=sem, vmem_limit_bytes=vmem)


def _proj_gla_kernel(x_ref, wg_ref, wlr_ref, wa_ref, ba_ref, q_ref, k_ref, v_ref, r_ref, g_ref):
    xb = x_ref[...].astype(BF16)
    q_ref[...] = _dot(xb, wg_ref[:, 0:GLA_QK])
    k_ref[...] = _dot(xb, wg_ref[:, GLA_QK:2 * GLA_QK])
    v_ref[...] = _dot(xb, wg_ref[:, 2 * GLA_QK:2 * GLA_QK + GLA_WIDTH]).astype(v_ref.dtype)
    r_ref[...] = _dot(xb, wg_ref[:, 2 * GLA_QK + GLA_WIDTH:])
    lr = _dot(xb, wlr_ref[...])
    z = _dot(lr.astype(BF16), wa_ref[...]) + ba_ref[...]
    log_sig = jnp.minimum(z, 0.0) - jnp.log1p(jnp.exp(-jnp.abs(z)))
    g_ref[...] = log_sig * (1.0 / GLA_TAU)


def _proj_gla(x, wg, wlr, wa, ba, tm):
    m = x.shape[0]
    const = lambda i: (0, 0)
    row = lambda i: (i, 0)
    return pl.pallas_call(
        _proj_gla_kernel,
        grid=(m // tm,),
        in_specs=[pl.BlockSpec((tm, D_MODEL), row),
                  pl.BlockSpec(wg.shape, const, pipeline_mode=pl.Buffered(1)),
                  pl.BlockSpec(wlr.shape, const),
                  pl.BlockSpec(wa.shape, const),
                  pl.BlockSpec(ba.shape, const)],
        out_specs=[pl.BlockSpec((tm, GLA_QK), row), pl.BlockSpec((tm, GLA_QK), row),
                   pl.BlockSpec((tm, GLA_WIDTH), row), pl.BlockSpec((tm, GLA_WIDTH), row),
                   pl.BlockSpec((tm, GLA_QK), row)],
        out_shape=[jax.ShapeDtypeStruct((m, GLA_QK), F32), jax.ShapeDtypeStruct((m, GLA_QK), F32),
                   jax.ShapeDtypeStruct((m, GLA_WIDTH), BF16), jax.ShapeDtypeStruct((m, GLA_WIDTH), F32),
                   jax.ShapeDtypeStruct((m, GLA_QK), F32)],
        compiler_params=_params(("parallel",)),
        name="proj_gla",
    )(x, wg, wlr, wa, ba)


def _proj_dswa_kernel(x_ref, w_ref, qb_ref, k_ref, v_ref, kb_ref, vb_ref):
    xb = x_ref[...].astype(BF16)
    w = DSWA_WIDTH
    qb_ref[...] = (_dot(xb, w_ref[:, 0:w]) * (DSWA_HEAD_DIM ** -0.5)).astype(BF16)
    k = _dot(xb, w_ref[:, w:2 * w])
    k_ref[...] = k
    kb_ref[...] = k.astype(BF16)
    v = _dot(xb, w_ref[:, 2 * w:3 * w])
    v_ref[...] = v
    vb_ref[...] = v.astype(BF16)


def _proj_dswa(x, w, tm):
    m = x.shape[0]
    row = lambda i: (i, 0)
    ospec = pl.BlockSpec((tm, DSWA_WIDTH), row)
    return pl.pallas_call(
        _proj_dswa_kernel,
        grid=(m // tm,),
        in_specs=[pl.BlockSpec((tm, D_MODEL), row),
                  pl.BlockSpec(w.shape, lambda i: (0, 0), pipeline_mode=pl.Buffered(1))],
        out_specs=[ospec] * 5,
        out_shape=[jax.ShapeDtypeStruct((m, DSWA_WIDTH), BF16),
                   jax.ShapeDtypeStruct((m, DSWA_WIDTH), F32),
                   jax.ShapeDtypeStruct((m, DSWA_WIDTH), F32),
                   jax.ShapeDtypeStruct((m, DSWA_WIDTH), BF16),
                   jax.ShapeDtypeStruct((m, DSWA_WIDTH), BF16)],
        compiler_params=_params(("parallel",)),
        name="proj_dswa",
    )(x, w)


def _gla_constants():
    c = GLA_CHUNK
    t = np.arange(c)
    tril = (t[None, :] <= t[:, None]).astype(np.float32)
    rows = [t, np.full(c, 31), 32 * (t // 32) + 15, 16 * (t // 16) + 7, np.full(c, c - 1)]
    cum = np.concatenate([tril[r] for r in rows], axis=0)

    def level_mask(half):
        blk = 2 * half
        return ((t[:, None] // blk == t[None, :] // blk) & (t[:, None] % blk >= half)
                & (t[None, :] % blk < half)).astype(np.float32)

    lmask = np.stack([level_mask(32), level_mask(16), level_mask(8)])
    dmask = np.stack([((t[None, :] == 8 * (t[:, None] // 8) + s) & (t[:, None] % 8 >= s)).astype(np.float32)
                      for s in range(8)])
    return jnp.asarray(cum, BF16), jnp.asarray(lmask), jnp.asarray(dmask)


def _gla_kernel(q_ref, k_ref, g_ref, v_ref, r_ref, nw_ref, s0_ref, cum_ref, lmask_ref, dmask_ref,
                o_ref, sout_ref, s_sc, *, n_chunks, n_steps, heads):
    step = pl.program_id(2)
    c = GLA_CHUNK

    @pl.when(step == 0)
    def _():
        s_sc[...] = s0_ref[0]

    for ci, hh in [(ci, hh) for ci in range(n_chunks) for hh in range(heads)]:
        rows = slice(ci * c, (ci + 1) * c)
        kcols = slice(hh * GLA_DK, (hh + 1) * GLA_DK)
        vcols = slice(hh * GLA_DV, (hh + 1) * GLA_DV)
        q = q_ref[rows, kcols] * (GLA_DK ** -0.5)
        k = k_ref[rows, kcols]
        g = g_ref[rows, kcols]
        v = v_ref[rows, vcols]
        g1 = g.astype(BF16)
        e1 = g - g1.astype(F32)
        g2 = e1.astype(BF16)
        g3 = (e1 - g2.astype(F32)).astype(BF16)
        cb = _dot(cum_ref[...], jnp.concatenate([g1, g2, g3], axis=1))
        cb = cb[:, 0:GLA_DK] + cb[:, GLA_DK:2 * GLA_DK] + cb[:, 2 * GLA_DK:3 * GLA_DK]
        b = cb[0:c]
        b_end = cb[4 * c:5 * c]
        state = s_sc[hh]
        o = _dot((q * jnp.exp(b)).astype(BF16), state.astype(BF16))
        att = jnp.zeros((c, c), F32)
        for li in range(3):
            bl = cb[(li + 1) * c:(li + 2) * c]
            ql = (q * jnp.exp(jnp.minimum(b - bl, 0.0))).astype(BF16)
            kl = (k * jnp.exp(jnp.minimum(bl - b, 0.0))).astype(BF16)
            att = att + _dot_nt(ql, kl) * lmask_ref[li]
        k3 = k.reshape(c // 8, 8, GLA_DK)
        b3 = b.reshape(c // 8, 8, GLA_DK)
        for sl in range(8):
            kb = jnp.broadcast_to(k3[:, sl:sl + 1, :], k3.shape).reshape(c, GLA_DK)
            bb = jnp.broadcast_to(b3[:, sl:sl + 1, :], b3.shape).reshape(c, GLA_DK)
            d = q * kb * jnp.exp(jnp.minimum(b - bb, 0.0))
            att = att + jnp.sum(d, axis=-1, keepdims=True) * dmask_ref[sl]
        o = o + _dot(att.astype(BF16), v)
        kd_t = (k * jnp.exp(b_end - b)).T.astype(BF16)
        e_col = jnp.exp(b_end).T[:, 0:1]
        s_sc[hh] = state * e_col + _dot(kd_t, v)
        on = o * lax.rsqrt(jnp.mean(o * o, axis=-1, keepdims=True) + RMS_EPS) * nw_ref[:, vcols]
        r = r_ref[rows, vcols]
        o_ref[rows, vcols] = (on * (r * jax.nn.sigmoid(r))).astype(o_ref.dtype)

    @pl.when(step == n_steps - 1)
    def _():
        sout_ref[0] = s_sc[...]


def _gla(q, k, g, v, r, nw, s0, batch, length, rows_per_step, heads):
    n_steps = length // rows_per_step
    cum, lmask, dmask = _gla_constants()
    qk_spec = pl.BlockSpec((rows_per_step, heads * GLA_DK), lambda b, h, t: (b * n_steps + t, h))
    v_spec = pl.BlockSpec((rows_per_step, heads * GLA_DV), lambda b, h, t: (b * n_steps + t, h))
    s_spec = pl.BlockSpec((1, heads, GLA_DK, GLA_DV), lambda b, h, t: (b, h, 0, 0))
    const2 = lambda b, h, t: (0, 0)
    const3 = lambda b, h, t: (0, 0, 0)
    kern = functools.partial(_gla_kernel, n_chunks=rows_per_step // GLA_CHUNK, n_steps=n_steps, heads=heads)
    return pl.pallas_call(
        kern,
        grid=(batch, GLA_HEADS // heads, n_steps),
        in_specs=[qk_spec, qk_spec, qk_spec, v_spec, v_spec,
                  pl.BlockSpec((1, heads * GLA_DV), lambda b, h, t: (0, h)),
                  s_spec,
                  pl.BlockSpec(cum.shape, const2),
                  pl.BlockSpec(lmask.shape, const3),
                  pl.BlockSpec(dmask.shape, const3)],
        out_specs=[v_spec, s_spec],
        out_shape=[jax.ShapeDtypeStruct((batch * length, GLA_WIDTH), BF16),
                   jax.ShapeDtypeStruct((batch, GLA_HEADS, GLA_DK, GLA_DV), F32)],
        scratch_shapes=[pltpu.VMEM((heads, GLA_DK, GLA_DV), F32)],
        compiler_params=_params(("parallel", "parallel", "arbitrary")),
        name="gla",
    )(q, k, g, v, r, nw, s0, cum, lmask, dmask)


def _alibi_slopes():
    return 2.0 ** (-8.0 * (np.arange(DSWA_HEADS, dtype=np.float64) + 1.0) / DSWA_HEADS)


def _branch_bias(dist):
    dist = np.asarray(dist, np.int64)
    mult = np.zeros(dist.shape, np.float64)
    for window, dil in DSWA_BRANCHES:
        mult += (dist >= 0) & (dist <= window) & (dist % dil == 0)
    slopes = _alibi_slopes().reshape((-1,) + (1,) * dist.ndim)
    bias = np.where(mult > 0, np.log(np.maximum(mult, 1.0)) - slopes * dist, NEG_BIG)
    return bias.astype(np.float32)


def _dswa_kernel(q_ref, k_ref, v_ref, bias_ref, o_ref, m_sc, l_sc, acc_sc, *, blk, heads):
    qi = pl.program_id(2)
    e = DSWA_HEAD_DIM
    m_sc[...] = jnp.full(m_sc.shape, NEG_BIG, F32)
    l_sc[...] = jnp.zeros(l_sc.shape, F32)
    acc_sc[...] = jnp.zeros(acc_sc.shape, F32)

    def body(kj, carry):
        off = pl.multiple_of(kj * blk, blk)
        for g in range(heads):
            cols = slice(g * e, (g + 1) * e)
            kb = k_ref[0, pl.ds(off, blk), cols]
            vb = v_ref[0, pl.ds(off, blk), cols]
            s = _dot_nt(q_ref[:, cols], kb) + bias_ref[g, qi - kj]
            m_prev = m_sc[g]
            m_new = jnp.maximum(m_prev, jnp.max(s, axis=-1, keepdims=True))
            p = jnp.exp(s - jnp.concatenate([m_new] * (blk // LANES), axis=1))
            alpha = jnp.exp(m_prev - m_new)
            l_sc[g] = alpha * l_sc[g] + jnp.sum(p, axis=-1, keepdims=True)
            acc_sc[g] = alpha * acc_sc[g] + _dot(p.astype(BF16), vb)
            m_sc[g] = m_new
        return carry

    lax.fori_loop(0, qi + 1, body, 0)
    for g in range(heads):
        o_ref[:, g * e:(g + 1) * e] = (acc_sc[g] / l_sc[g]).astype(o_ref.dtype)


def _dswa_prompt(qb, kb, vb, batch, seq, blk, heads):
    nq = seq // blk
    t = np.arange(blk)
    dist = (np.arange(nq)[:, None, None] * blk + t[None, :, None] - t[None, None, :])
    bias = jnp.asarray(_branch_bias(dist))
    k3 = kb.reshape(batch, seq, DSWA_WIDTH)
    v3 = vb.reshape(batch, seq, DSWA_WIDTH)
    width = heads * DSWA_HEAD_DIM
    q_spec = pl.BlockSpec((blk, width), lambda h, b, i: (b * nq + i, h))
    kv_spec = pl.BlockSpec((1, seq, width), lambda h, b, i: (b, 0, h))
    stat = pltpu.VMEM((heads, blk, DSWA_HEAD_DIM), F32)
    return pl.pallas_call(
        functools.partial(_dswa_kernel, blk=blk, heads=heads),
        grid=(DSWA_HEADS // heads, batch, nq),
        in_specs=[q_spec, kv_spec, kv_spec,
                  pl.BlockSpec((heads, nq, blk, blk), lambda h, b, i: (h, 0, 0, 0),
                               pipeline_mode=pl.Buffered(1))],
        out_specs=q_spec,
        out_shape=jax.ShapeDtypeStruct((batch * seq, DSWA_WIDTH), BF16),
        scratch_shapes=[stat, stat, stat],
        compiler_params=_params(("parallel", "parallel", "arbitrary")),
        name="dswa_prompt",
    )(qb, k3, v3, bias)


def _dswa_sample_kernel(q_ref, ka_ref, va_ref, kb_ref, vb_ref, kn_ref, vn_ref, ba_ref, bb_ref, bn_ref,
                        o_ref):
    e = DSWA_HEAD_DIM
    q = q_ref[0]
    flat = lambda ref_val: ref_val.reshape(-1, e).astype(BF16)
    sa = _dot_nt(q, flat(ka_ref[0, 0])) + ba_ref[...]
    sb = _dot_nt(q, flat(kb_ref[0])) + bb_ref[...]
    sn = _dot_nt(q, kn_ref[0]) + bn_ref[...]
    rowmax = lambda s: jnp.max(s, axis=-1, keepdims=True)
    m = jnp.maximum(jnp.maximum(rowmax(sa), rowmax(sb)), rowmax(sn))
    pa = jnp.exp(sa - m)
    pb = jnp.exp(sb - m)
    pn = jnp.exp(sn - m)
    rowsum = lambda p: jnp.sum(p, axis=-1, keepdims=True)
    den = rowsum(pa) + rowsum(pb) + rowsum(pn)
    out = (_dot(pa.astype(BF16), flat(va_ref[0, 0])) + _dot(pb.astype(BF16), flat(vb_ref[0]))
           + _dot(pn.astype(BF16), vn_ref[0]))
    o_ref[0] = out / den


def _cross_head_bias(bias):
    h, n_new, n_keys = bias.shape
    full = np.full((h, n_new, n_keys, h), NEG_BIG, np.float32)
    for hh in range(h):
        full[hh, :, :, hh] = bias[hh]
    return full.reshape(h * n_new, n_keys * h)


def _dswa_sample(qb, k_new, v_new, cache_k, cache_v):
    batch, n_new, h, e = qb.shape
    n_past = cache_k.shape[1]
    tail = DSWA_BRANCHES[1][0]
    wide_window, wide_dil = DSWA_BRANCHES[2]
    assert DSWA_BRANCHES[0][0] <= tail
    assert n_past % tail == 0 and n_past % wide_dil == 0 and tail % wide_dil == 0 and n_new <= wide_dil
    n_groups = (n_past - tail) // wide_dil
    rows = h * n_new
    lq = np.arange(n_new)
    pos_a = n_past - tail + np.arange(tail)
    pos_b = (wide_dil * np.arange(n_groups)[:, None] + np.arange(n_new)[None, :]).reshape(-1)
    bias_a = _cross_head_bias(_branch_bias(n_past + lq[:, None] - pos_a[None, :]))
    bias_b = _cross_head_bias(_branch_bias(n_past + lq[:, None] - pos_b[None, :]))
    bias_n = np.full((rows, LANES), NEG_BIG, np.float32)
    bias_n[:, :n_new * h] = _cross_head_bias(_branch_bias(lq[:, None] - lq[None, :]))
    q2 = qb.transpose(0, 2, 1, 3).reshape(batch, rows, e)
    pad = ((0, 0), (0, LANES - n_new * h), (0, 0))
    kn = jnp.pad(k_new.reshape(batch, n_new * h, e).astype(BF16), pad)
    vn = jnp.pad(v_new.reshape(batch, n_new * h, e).astype(BF16), pad)
    tail_view = lambda c: c.reshape(batch, n_past // tail, tail, h, e)
    wide_view = lambda c: c.reshape(batch, n_past // wide_dil, wide_dil, h, e)
    tail_spec = pl.BlockSpec((1, 1, tail, h, e), lambda b: (b, n_past // tail - 1, 0, 0, 0))
    wide_spec = pl.BlockSpec((1, n_groups, n_new, h, e), lambda b: (b, 0, 0, 0, 0))
    b3 = lambda b: (b, 0, 0)
    c2 = lambda b: (0, 0)
    out = pl.pallas_call(
        _dswa_sample_kernel,
        grid=(batch,),
        in_specs=[pl.BlockSpec((1, rows, e), b3), tail_spec, tail_spec, wide_spec, wide_spec,
                  pl.BlockSpec((1, LANES, e), b3), pl.BlockSpec((1, LANES, e), b3),
                  pl.BlockSpec(bias_a.shape, c2), pl.BlockSpec(bias_b.shape, c2),
                  pl.BlockSpec(bias_n.shape, c2)],
        out_specs=pl.BlockSpec((1, rows, e), b3),
        out_shape=jax.ShapeDtypeStruct((batch, rows, e), F32),
        compiler_params=_params(("parallel",)),
        name="dswa_sample",
    )(q2, tail_view(cache_k), tail_view(cache_v), wide_view(cache_k), wide_view(cache_v), kn, vn,
      jnp.asarray(bias_a), jnp.asarray(bias_b), jnp.asarray(bias_n))
    return out.reshape(batch, h, n_new, e).transpose(0, 2, 1, 3).reshape(batch, n_new, h * e)


def _layer_norm(pre, g, b):
    mu = jnp.mean(pre, axis=-1, keepdims=True)
    cen = pre - mu
    var = jnp.mean(cen * cen, axis=-1, keepdims=True)
    return cen * lax.rsqrt(var + LN_EPS) * g + b


def _outproj_kernel(ogp_ref, odp_ref, xp_ref, ogs_ref, ods_ref, xs_ref, wo_ref, g_ref, b_ref,
                    wr_ref, br_ref, tri_ref, h_ref, route_ref, cnt_ref, cnt_sc, *, n_prompt_tiles):
    i = pl.program_id(0)
    is_s = i >= n_prompt_tiles

    @pl.when(i == 0)
    def _():
        cnt_sc[...] = jnp.zeros(cnt_sc.shape, F32)

    og = jnp.where(is_s, ogs_ref[...], ogp_ref[...])
    od = jnp.where(is_s, ods_ref[...], odp_ref[...])
    x = jnp.where(is_s, xs_ref[...], xp_ref[...])
    acc = _dot(og, wo_ref[0:GLA_WIDTH, :]) + _dot(od, wo_ref[GLA_WIDTH:, :])
    h = _layer_norm(DEEPNORM_ALPHA * x + acc, g_ref[...], b_ref[...])
    h_ref[...] = h
    logits = _dot(h.astype(BF16), wr_ref[...]) + br_ref[...]
    lane = lax.broadcasted_iota(I32, logits.shape, 1).astype(F32)
    tops, sels, idxs = [], [], []
    cur = logits
    for _ in range(TOP_K):
        mx = jnp.max(cur, axis=-1, keepdims=True)
        idx = jnp.min(jnp.where(cur == mx, lane, float(LANES)), axis=-1, keepdims=True)
        sel = lane == idx
        tops.append(mx)
        sels.append(sel)
        idxs.append(idx)
        cur = jnp.where(sel, -jnp.inf, cur)
    exps = [jnp.exp(t - tops[0]) for t in tops]
    den = exps[0] + exps[1] + exps[2] + exps[3]
    multi = jnp.zeros(logits.shape, F32)
    for sel in sels:
        multi = multi + jnp.where(sel, 1.0, 0.0)
    rank_all = _dot(tri_ref[...], multi.astype(BF16)) + cnt_sc[...]
    cnt_sc[...] = cnt_sc[...] + jnp.sum(multi, axis=0, keepdims=True)
    cnt_ref[...] = cnt_sc[...]
    route = jnp.zeros(logits.shape, F32)
    for kk in range(TOP_K):
        rank = jnp.sum(jnp.where(sels[kk], rank_all, 0.0), axis=-1, keepdims=True)
        route = jnp.where(lane == kk, idxs[kk], route)
        route = jnp.where(lane == TOP_K + kk, exps[kk] / den, route)
        route = jnp.where(lane == 2 * TOP_K + kk, rank, route)
    route_ref[...] = route


def _outproj(og_p, od_p, x_p, og_s, od_s, x_s, wo, g, b, wr, br, tm):
    n_p = x_p.shape[0] // tm
    tri = jnp.asarray(np.tril(np.ones((tm, tm), np.float32), -1), BF16)
    prow = lambda i: (jnp.minimum(i, n_p - 1), 0)
    c2 = lambda i: (0, 0)
    row = lambda i: (i, 0)
    m_tot = x_p.shape[0] + tm
    return pl.pallas_call(
        functools.partial(_outproj_kernel, n_prompt_tiles=n_p),
        grid=(n_p + 1,),
        in_specs=[pl.BlockSpec((tm, GLA_WIDTH), prow), pl.BlockSpec((tm, DSWA_WIDTH), prow),
                  pl.BlockSpec((tm, D_MODEL), prow),
                  pl.BlockSpec((tm, GLA_WIDTH), c2), pl.BlockSpec((tm, DSWA_WIDTH), c2),
                  pl.BlockSpec((tm, D_MODEL), c2),
                  pl.BlockSpec(wo.shape, c2), pl.BlockSpec(g.shape, c2), pl.BlockSpec(b.shape, c2),
                  pl.BlockSpec(wr.shape, c2), pl.BlockSpec(br.shape, c2), pl.BlockSpec(tri.shape, c2)],
        out_specs=[pl.BlockSpec((tm, D_MODEL), row), pl.BlockSpec((tm, LANES), row),
                   pl.BlockSpec((1, LANES), c2)],
        out_shape=[jax.ShapeDtypeStruct((m_tot, D_MODEL), F32),
                   jax.ShapeDtypeStruct((m_tot, LANES), F32),
                   jax.ShapeDtypeStruct((1, LANES), F32)],
        scratch_shapes=[pltpu.VMEM((1, LANES), F32)],
        compiler_params=_params(("arbitrary",)),
        name="outproj_ln_router",
    )(og_p, od_p, x_p, og_s, od_s, x_s, wo, g, b, wr, br, tri)


def _dispatch_kernel(cnt_ref, pst_ref, dest_ref, h_hbm, xr_ref, hbuf, zero_sc, lsems, sems, zsem, *,
                     tm, tb, n_blocks, n_steps):
    i = pl.program_id(0)
    slot = i % 2
    ring = i % 3

    def tile_load(step):
        rows = pl.ds(pl.multiple_of(step * tm, tm), tm)
        return pltpu.make_async_copy(h_hbm.at[rows, :], hbuf.at[step % 3], lsems.at[step % 3])

    @pl.when(i == 0)
    def _():
        tile_load(0).start()

    @pl.when(i + 1 < n_steps)
    def _():
        tile_load(i + 1).start()

    tile_load(i).wait()
    h_ref = hbuf.at[ring]

    def row_copy(t, r, s):
        return pltpu.make_async_copy(h_ref.at[pl.ds(t, 1), :], xr_ref.at[pl.ds(r, 1), :], sems.at[s])

    def issue(t, carry):
        for kk in range(TOP_K):
            row_copy(t, dest_ref[0, 0, t * TOP_K + kk], slot).start(priority=kk % 2)
        return carry

    lax.fori_loop(0, tm, issue, 0)

    def zero_copy(r):
        return pltpu.make_async_copy(zero_sc.at[pl.ds(0, 1), :], xr_ref.at[pl.ds(r, 1), :], zsem)

    @pl.when(i == 0)
    def _():
        zero_sc[...] = jnp.zeros(zero_sc.shape, F32)

        def per_expert(e, carry):
            cnt = cnt_ref[e]
            base = pst_ref[e]
            end = (cnt + tb - 1) // tb * tb

            def start(r, c2):
                zero_copy(base + r).start()
                return c2

            def wait(r, c2):
                zero_copy(base + r).wait()
                return c2

            lax.fori_loop(cnt, end, start, 0)
            lax.fori_loop(cnt, end, wait, 0)
            return carry

        lax.fori_loop(0, N_EXPERTS, per_expert, 0)

        last = N_EXPERTS - 1
        n_used = (pst_ref[last] + (cnt_ref[last] + tb - 1) // tb * tb) // tb

        def tail_copy(blk):
            return pltpu.make_async_copy(zero_sc, xr_ref.at[pl.ds(pl.multiple_of(blk * tb, tb), tb), :], zsem)

        def tail_start(blk, c2):
            tail_copy(blk).start()
            return c2

        def tail_wait(blk, c2):
            tail_copy(blk).wait()
            return c2

        lax.fori_loop(n_used, n_blocks, tail_start, 0)
        lax.fori_loop(n_used, n_blocks, tail_wait, 0)

    def drain(s):
        def body(t, carry):
            for kk in range(TOP_K):
                row_copy(t, 0, s).wait()
            return carry
        lax.fori_loop(0, tm, body, 0)

    @pl.when(i > 0)
    def _():
        drain(1 - slot)

    @pl.when(i == n_steps - 1)
    def _():
        drain(slot)


def _dispatch(h, dest, counts, pstart, n_rows, tm, tb):
    m = h.shape[0]
    n_t = m // tm
    dest3 = dest.reshape(n_t, 1, tm * TOP_K)
    grid_spec = pltpu.PrefetchScalarGridSpec(
        num_scalar_prefetch=2,
        grid=(n_t,),
        in_specs=[pl.BlockSpec((1, 1, tm * TOP_K), lambda i, c, p: (i, 0, 0), memory_space=pltpu.SMEM),
                  pl.BlockSpec(memory_space=pl.ANY)],
        out_specs=pl.BlockSpec(memory_space=pl.ANY),
        scratch_shapes=[pltpu.VMEM((3, tm, D_MODEL), F32), pltpu.VMEM((tb, D_MODEL), F32),
                        pltpu.SemaphoreType.DMA((3,)), pltpu.SemaphoreType.DMA((2,)),
                        pltpu.SemaphoreType.DMA(())],
    )
    return pl.pallas_call(
        functools.partial(_dispatch_kernel, tm=tm, tb=tb, n_blocks=n_rows // tb, n_steps=n_t),
        grid_spec=grid_spec,
        out_shape=jax.ShapeDtypeStruct((n_rows, D_MODEL), F32),
        compiler_params=_params(("arbitrary",)),
        name="moe_dispatch",
    )(counts, pstart, dest3, h)


def _staged_expert_kernel(be_ref, nvb_ref, lo_ref, hi_ref, kick_ref, nxt_ref, par_ref, x_ref, *rest,
                          n_mats, compute):
    w_refs = rest[:n_mats]
    b_refs = rest[n_mats:2 * n_mats]
    o_ref = rest[2 * n_mats]
    wb, stage, sems = rest[2 * n_mats + 1:]
    i = pl.program_id(0)
    n_chunks = wb.shape[2] // MOE_W_CHUNK

    def chunk_rows(c):
        return pl.ds(pl.multiple_of(c * MOE_W_CHUNK, MOE_W_CHUNK), MOE_W_CHUNK)

    def copies(e, c):
        slot = c % 2
        return [pltpu.make_async_copy(w.at[e, chunk_rows(c), :], stage.at[slot, mi], sems.at[slot])
                for mi, w in enumerate(w_refs)]

    def start(e, c):
        for cp in copies(e, c):
            cp.start()

    def finish_range(e, lo, hi, dst):
        def body(c, carry):
            for cp in copies(e, c):
                cp.wait()
            for mi in range(n_mats):
                wb[dst, mi, chunk_rows(c), :] = stage[c % 2, mi].astype(BF16)

            @pl.when(c + 2 < n_chunks)
            def _():
                start(e, c + 2)

            return carry

        lax.fori_loop(lo, hi, body, 0)

    cur = par_ref[i]

    @pl.when(i == 0)
    def _():
        start(be_ref[0], 0)
        start(be_ref[0], 1)
        finish_range(be_ref[0], 0, n_chunks, cur)

    @pl.when(kick_ref[i] == 1)
    def _():
        start(nxt_ref[i], 0)
        start(nxt_ref[i], 1)

    @pl.when(i < nvb_ref[0])
    def _():
        compute(x_ref, [wb.at[cur, mi] for mi in range(n_mats)], b_refs, o_ref)

    @pl.when(i >= nvb_ref[0])
    def _():
        o_ref[...] = jnp.zeros(o_ref.shape, o_ref.dtype)

    finish_range(nxt_ref[i], lo_ref[i], hi_ref[i], 1 - cur)


def _up_compute(x_ref, w, b_refs, a_ref):
    xb = x_ref[...].astype(BF16)
    half = a_ref.shape[1] // 2
    for n in range(2):
        cols = slice(n * half, (n + 1) * half)
        gate = jnp.minimum(_dot(xb, w[0][:, cols]) + b_refs[0][0, :, cols], SWIGLU_LIMIT)
        up = jnp.clip(_dot(xb, w[1][:, cols]) + b_refs[1][0, :, cols], -SWIGLU_LIMIT, SWIGLU_LIMIT)
        a_ref[:, cols] = (gate * jax.nn.sigmoid(SWIGLU_ALPHA * gate) * (up + 1.0)).astype(a_ref.dtype)


def _down_compute(a_ref, w, b_refs, y_ref):
    ab = a_ref[...]
    half = y_ref.shape[2] // 2
    for n in range(2):
        cols = slice(n * half, (n + 1) * half)
        y_ref[:, 0, cols] = _dot(ab, w[0][:, cols]) + b_refs[0][0, :, cols]


def _staged_expert_call(x_rows, meta, weights, biases, out_dtype, compute, name, tb, row_major_out=False):
    n_rows = x_rows.shape[0]
    k, n_out = weights[0].shape[1:]
    n_mats = len(weights)
    x_block = (tb,) + x_rows.shape[1:]
    rows_map = lambda i, be, nv, *_: (jnp.minimum(i, nv[0] - 1),) + (0,) * (len(x_block) - 1)
    bias_spec = pl.BlockSpec((1, 1, n_out), lambda i, be, *_: (be[i], 0, 0))
    if row_major_out:
        out_shape, out_spec = (n_rows, 1, n_out), pl.BlockSpec((tb, 1, n_out), lambda i, *_: (i, 0, 0))
    else:
        out_shape, out_spec = (n_rows, n_out), pl.BlockSpec((tb, n_out), lambda i, *_: (i, 0))
    grid_spec = pltpu.PrefetchScalarGridSpec(
        num_scalar_prefetch=len(meta),
        grid=(n_rows // tb,),
        in_specs=([pl.BlockSpec(x_block, rows_map)] + [pl.BlockSpec(memory_space=pl.ANY)] * n_mats
                  + [bias_spec] * n_mats),
        out_specs=out_spec,
        scratch_shapes=[pltpu.VMEM((2, n_mats, k, n_out), BF16),
                        pltpu.VMEM((2, n_mats, MOE_W_CHUNK, n_out), F32),
                        pltpu.SemaphoreType.DMA((2,))],
    )
    return pl.pallas_call(
        functools.partial(_staged_expert_kernel, n_mats=n_mats, compute=compute),
        grid_spec=grid_spec,
        out_shape=jax.ShapeDtypeStruct(out_shape, out_dtype),
        compiler_params=_params(("arbitrary",)),
        name=name,
    )(*meta, x_rows, *weights, *[b.reshape(N_EXPERTS, 1, n_out) for b in biases])


def _expert_schedule(blk_e, nvb, padded, pstart, tb, n_chunks):
    nb = blk_e.shape[0]
    ids = jnp.arange(N_EXPERTS, dtype=I32)
    n_blk = padded // tb
    has = n_blk > 0
    later = jnp.where(has[None, :] & (ids[None, :] > ids[:, None]), ids[None, :], N_EXPERTS)
    nxt_e = jnp.min(later, axis=1)
    nxt_e = jnp.where(nxt_e == N_EXPERTS, -1, nxt_e)
    par_e = jnp.maximum(jnp.cumsum(has.astype(I32)) - 1, 0) % 2
    bi = jnp.arange(nb, dtype=I32)
    rank = bi - (pstart // tb)[blk_e]
    n_e = jnp.maximum(n_blk[blk_e], 1)
    active = (bi < nvb[0]) & (nxt_e[blk_e] >= 0)
    lo = jnp.where(active, n_chunks * rank // n_e, 0)
    hi = jnp.where(active, n_chunks * (rank + 1) // n_e, 0)
    kick = (active & (rank == 0)).astype(I32)
    to_i32 = lambda a: a.astype(I32)
    return tuple(map(to_i32, (blk_e, nvb, lo, hi, kick, nxt_e[blk_e], par_e[blk_e])))


def _combine_kernel(dcur_ref, dnxt_ref, y_ref, h_ref, gate_ref, g_ref, b_ref, yp_ref, ys_ref,
                    buf, sems, *, tm, n_tiles, n_prompt_tiles):
    i = pl.program_id(0)
    slot = i % 2

    def row_copy(dref, t, kk, sl):
        r = dref[0, 0, t * TOP_K + kk]
        return pltpu.make_async_copy(y_ref.at[pl.ds(r, 1)], buf.at[sl, kk, pl.ds(t, 1)], sems.at[sl])

    def issue(dref, sl):
        def body(t, carry):
            for kk in range(TOP_K):
                row_copy(dref, t, kk, sl).start(priority=kk % 2)
            return carry
        lax.fori_loop(0, tm, body, 0)

    @pl.when(i == 0)
    def _():
        issue(dcur_ref, 0)

    @pl.when(i + 1 < n_tiles)
    def _():
        issue(dnxt_ref, 1 - slot)

    def drain(t, carry):
        for kk in range(TOP_K):
            row_copy(dcur_ref, t, kk, slot).wait()
        return carry

    lax.fori_loop(0, tm, drain, 0)
    gate = gate_ref[...]
    f = jnp.zeros((tm, D_MODEL), F32)
    for kk in range(TOP_K):
        f = f + buf[slot, kk, :, 0, :] * gate[:, TOP_K + kk:TOP_K + kk + 1]
    y = _layer_norm(DEEPNORM_ALPHA * h_ref[...] + f, g_ref[...], b_ref[...])

    @pl.when(i < n_prompt_tiles)
    def _():
        yp_ref[...] = y

    @pl.when(i >= n_prompt_tiles)
    def _():
        ys_ref[...] = y


def _combine(y_rows, h, route, dest, g, b, n_prompt, n_sample, tm):
    n_p = n_prompt // tm
    n_tiles = n_p + n_sample // tm
    dest3 = dest.reshape(-1, 1, tm * TOP_K)
    smem = functools.partial(pl.BlockSpec, (1, 1, tm * TOP_K), memory_space=pltpu.SMEM)
    c2 = lambda i: (0, 0)
    return pl.pallas_call(
        functools.partial(_combine_kernel, tm=tm, n_tiles=n_tiles, n_prompt_tiles=n_p),
        grid=(n_tiles,),
        in_specs=[smem(lambda i: (i, 0, 0)),
                  smem(lambda i: (jnp.minimum(i + 1, n_tiles - 1), 0, 0)),
                  pl.BlockSpec(memory_space=pl.ANY),
                  pl.BlockSpec((tm, D_MODEL), lambda i: (i, 0)),
                  pl.BlockSpec((tm, LANES), lambda i: (i, 0)),
                  pl.BlockSpec(g.shape, c2), pl.BlockSpec(b.shape, c2)],
        out_specs=[pl.BlockSpec((tm, D_MODEL), lambda i: (jnp.minimum(i, n_p - 1), 0)),
                   pl.BlockSpec((tm, D_MODEL), lambda i: (jnp.maximum(i - n_p, 0), 0))],
        out_shape=[jax.ShapeDtypeStruct((n_prompt, D_MODEL), F32),
                   jax.ShapeDtypeStruct((n_sample, D_MODEL), F32)],
        scratch_shapes=[pltpu.VMEM((2, TOP_K, tm, 1, D_MODEL), F32), pltpu.SemaphoreType.DMA((2,))],
        compiler_params=_params(("arbitrary",)),
        name="moe_combine_ln",
    )(dest3, dest3, y_rows, h, route, g, b)


def _moe_and_norm(h, route, cnt, n_prompt, n_sample, w_exp_gate, b_exp_gate, w_exp_up, b_exp_up,
                  w_exp_down, b_exp_down, ln2_g, ln2_b, tb, out_tm, comb_tm):
    m_tot = h.shape[0]
    top_e = route[:, 0:TOP_K].astype(I32)
    rank = route[:, 2 * TOP_K:3 * TOP_K].astype(I32)
    counts = cnt[0, 0:N_EXPERTS].astype(I32)
    padded = (counts + tb - 1) // tb * tb
    pend = jnp.cumsum(padded)
    pstart = pend - padded
    dest = (pstart[top_e] + rank).reshape(-1)
    nb = (m_tot * TOP_K + N_EXPERTS * (tb - 1) + tb - 1) // tb
    nvb = (pend[-1:] // tb).astype(I32)
    blk_start = jnp.arange(nb, dtype=I32) * tb
    blk_e = jnp.minimum(jnp.sum((blk_start[:, None] >= pend[None, :]).astype(I32), axis=1), N_EXPERTS - 1)
    meta = _expert_schedule(blk_e, nvb, padded, pstart, tb, D_MODEL // MOE_W_CHUNK)
    x_rows = _dispatch(h, dest, counts, pstart.astype(I32), nb * tb, out_tm, tb)
    a_rows = _staged_expert_call(x_rows, meta, (w_exp_gate, w_exp_up), (b_exp_gate, b_exp_up), BF16,
                                 _up_compute, "moe_up", tb)
    y_rows = _staged_expert_call(a_rows, meta, (w_exp_down,), (b_exp_down,), F32,
                                 _down_compute, "moe_down", tb, row_major_out=True)
    return _combine(y_rows, h, route, dest, ln2_g, ln2_b, n_prompt, n_sample, comb_tm)


def _split_w_in(w_in):
    sizes = (GLA_QK, GLA_QK, GLA_WIDTH, GLA_GATE_RANK, GLA_WIDTH, DSWA_WIDTH, DSWA_WIDTH, DSWA_WIDTH)
    o = np.cumsum((0,) + sizes)
    wg = jnp.concatenate([w_in[:, o[0]:o[3]], w_in[:, o[4]:o[5]]], axis=1).astype(BF16)
    wlr = jnp.pad(w_in[:, o[3]:o[4]], ((0, 0), (0, LANES - GLA_GATE_RANK))).astype(BF16)
    wd = w_in[:, o[5]:o[8]].astype(BF16)
    return wg, wlr, wd


def kernel(x_prompt, x_sample, state_gla, cache_dswa_k, cache_dswa_v, w_in, w_gla_alpha, b_gla_alpha,
           gla_norm_w, w_o, ln1_g, ln1_b, w_router, b_router, w_exp_gate, b_exp_gate, w_exp_up,
           b_exp_up, w_exp_down, b_exp_down, ln2_g, ln2_b):
    bp, sp, _ = x_prompt.shape
    bs, ls, _ = x_sample.shape
    n_prompt, n_sample = bp * sp, bs * ls
    n_past = cache_dswa_k.shape[1]

    wg, wlr, wd = _split_w_in(w_in)
    wa = jnp.pad(w_gla_alpha, ((0, LANES - GLA_GATE_RANK), (0, 0))).astype(BF16)
    ba = b_gla_alpha.reshape(1, GLA_QK)
    nw = gla_norm_w.reshape(1, GLA_WIDTH)
    xp = x_prompt.reshape(n_prompt, D_MODEL)
    xs = x_sample.reshape(n_sample, D_MODEL)

    q, k, v, r, g = _proj_gla(xp, wg, wlr, wa, ba, PROJ_TM)
    og_p, state_p = _gla(q, k, g, v, r, nw, jnp.zeros((bp, GLA_HEADS, GLA_DK, GLA_DV), F32),
                         bp, sp, GLA_ROWS, GLA_STEP_HEADS)
    dqb, dk, dv, dkb, dvb = _proj_dswa(xp, wd, PROJ_TM)
    od_p = _dswa_prompt(dqb, dkb, dvb, bp, sp, ATT_T, ATT_HEADS)

    qs, ks, vs, rs, gs = _proj_gla(xs, wg, wlr, wa, ba, n_sample)
    lpad = GLA_CHUNK - ls
    pad_rows = lambda a: jnp.pad(a.reshape(bs, ls, -1), ((0, 0), (0, lpad), (0, 0))).reshape(bs * GLA_CHUNK, -1)
    og_s_pad, state_s = _gla(pad_rows(qs), pad_rows(ks), pad_rows(gs), pad_rows(vs), pad_rows(rs), nw,
                             state_gla, bs, GLA_CHUNK, GLA_CHUNK, GLA_STEP_HEADS)
    og_s = og_s_pad.reshape(bs, GLA_CHUNK, GLA_WIDTH)[:, :ls].reshape(n_sample, GLA_WIDTH)
    tqb, tk, tv, _, _ = _proj_dswa(xs, wd, n_sample)
    heads4 = lambda a: a.reshape(bs, ls, DSWA_HEADS, DSWA_HEAD_DIM)
    od_s = _dswa_sample(heads4(tqb), heads4(tk), heads4(tv), cache_dswa_k, cache_dswa_v)
    od_s = od_s.reshape(n_sample, DSWA_WIDTH).astype(BF16)

    spad = ((0, OUT_TM - n_sample), (0, 0))
    wr = jnp.pad(w_router, ((0, 0), (0, LANES - N_EXPERTS))).astype(BF16)
    br = jnp.pad(b_router, (0, LANES - N_EXPERTS), constant_values=NEG_BIG).reshape(1, LANES)
    h, route, cnt = _outproj(og_p, od_p, xp, jnp.pad(og_s, spad), jnp.pad(od_s, spad), jnp.pad(xs, spad),
                             w_o.astype(BF16), ln1_g.reshape(1, D_MODEL), ln1_b.reshape(1, D_MODEL),
                             wr, br, OUT_TM)
    y_p, y_s = _moe_and_norm(h, route, cnt, n_prompt, n_sample, w_exp_gate, b_exp_gate, w_exp_up,
                             b_exp_up, w_exp_down, b_exp_down, ln2_g.reshape(1, D_MODEL),
                             ln2_b.reshape(1, D_MODEL), MOE_TB, OUT_TM, COMB_TM)

    return (y_p.reshape(bp, sp, D_MODEL), y_s.reshape(bs, ls, D_MODEL), state_p,
            dk.reshape(bp, sp, DSWA_HEADS, DSWA_HEAD_DIM), dv.reshape(bp, sp, DSWA_HEADS, DSWA_HEAD_DIM),
            state_s, tk.reshape(bs, ls, DSWA_HEADS, DSWA_HEAD_DIM),
            tv.reshape(bs, ls, DSWA_HEADS, DSWA_HEAD_DIM))
```

```python
import functools

import numpy as np
import jax
import jax.numpy as jnp
from jax import lax
from jax.experimental import pallas as pl
from jax.experimental.pallas import tpu as pltpu

F32 = jnp.float32
BF16 = jnp.bfloat16
I32 = jnp.int32

D_MODEL = 2048
GLA_HEADS = 4
GLA_DK = 128
GLA_DV = 256
GLA_WIDTH = GLA_HEADS * GLA_DV
GLA_QK = GLA_HEADS * GLA_DK
GLA_GATE_RANK = 16
GLA_TAU = 16.0
GLA_CHUNK = 64
DSWA_HEADS = 8
DSWA_HEAD_DIM = 128
DSWA_WIDTH = DSWA_HEADS * DSWA_HEAD_DIM
DSWA_BRANCHES = ((128, 1), (512, 4), (2048, 16))
N_EXPERTS = 32
TOP_K = 4
D_FF = D_MODEL
SWIGLU_LIMIT = 7.0
SWIGLU_ALPHA = 1.702
DEEPNORM_ALPHA = 2.0 ** 0.25
LN_EPS = 1e-5
RMS_EPS = 1e-6
NEG_BIG = -1e30

LANES = 128
V7X_VMEM_LIMIT = 56 * 1024 * 1024

PROJ_TM = 512
GLA_ROWS = 256
GLA_STEP_HEADS = 2
ATT_T = 256
ATT_HEADS = 8
OUT_TM = 256
MOE_TB = 256
MOE_W_CHUNK = 128
COMB_TM = 128


def _dot(a, b):
    return jnp.dot(a, b, preferred_element_type=F32)


def _dot_nt(a, b):
    return lax.dot_general(a, b, (((1,), (1,)), ((), ())), preferred_element_type=F32)


def _params(sem, vmem=V7X_VMEM_LIMIT):
    return pltpu.CompilerParams(dimension_semantics=sem, vmem_limit_bytes=vmem)


def _proj_gla_kernel(x_ref, wg_ref, wlr_ref, wa_ref, ba_ref, q_ref, k_ref, v_ref, r_ref, g_ref):
    xb = x_ref[...].astype(BF16)
    q_ref[...] = _dot(xb, wg_ref[:, 0:GLA_QK])
    k_ref[...] = _dot(xb, wg_ref[:, GLA_QK:2 * GLA_QK])
    v_ref[...] = _dot(xb, wg_ref[:, 2 * GLA_QK:2 * GLA_QK + GLA_WIDTH]).astype(v_ref.dtype)
    r_ref[...] = _dot(xb, wg_ref[:, 2 * GLA_QK + GLA_WIDTH:])
    lr = _dot(xb, wlr_ref[...])
    z = _dot(lr.astype(BF16), wa_ref[...]) + ba_ref[...]
    log_sig = jnp.minimum(z, 0.0) - jnp.log1p(jnp.exp(-jnp.abs(z)))
    g_ref[...] = log_sig * (1.0 / GLA_TAU)


def _proj_gla(x, wg, wlr, wa, ba, tm):
    m = x.shape[0]
    const = lambda i: (0, 0)
    row = lambda i: (i, 0)
    return pl.pallas_call(
        _proj_gla_kernel,
        grid=(m // tm,),
        in_specs=[pl.BlockSpec((tm, D_MODEL), row),
                  pl.BlockSpec(wg.shape, const, pipeline_mode=pl.Buffered(1)),
                  pl.BlockSpec(wlr.shape, const),
                  pl.BlockSpec(wa.shape, const),
                  pl.BlockSpec(ba.shape, const)],
        out_specs=[pl.BlockSpec((tm, GLA_QK), row), pl.BlockSpec((tm, GLA_QK), row),
                   pl.BlockSpec((tm, GLA_WIDTH), row), pl.BlockSpec((tm, GLA_WIDTH), row),
                   pl.BlockSpec((tm, GLA_QK), row)],
        out_shape=[jax.ShapeDtypeStruct((m, GLA_QK), F32), jax.ShapeDtypeStruct((m, GLA_QK), F32),
                   jax.ShapeDtypeStruct((m, GLA_WIDTH), BF16), jax.ShapeDtypeStruct((m, GLA_WIDTH), F32),
                   jax.ShapeDtypeStruct((m, GLA_QK), F32)],
        compiler_params=_params(("parallel",)),
        name="proj_gla",
    )(x, wg, wlr, wa, ba)


def _proj_dswa_kernel(x_ref, w_ref, qb_ref, k_ref, v_ref, kb_ref, vb_ref):
    xb = x_ref[...].astype(BF16)
    w = DSWA_WIDTH
    qb_ref[...] = (_dot(xb, w_ref[:, 0:w]) * (DSWA_HEAD_DIM ** -0.5)).astype(BF16)
    k = _dot(xb, w_ref[:, w:2 * w])
    k_ref[...] = k
    kb_ref[...] = k.astype(BF16)
    v = _dot(xb, w_ref[:, 2 * w:3 * w])
    v_ref[...] = v
    vb_ref[...] = v.astype(BF16)


def _proj_dswa(x, w, tm):
    m = x.shape[0]
    row = lambda i: (i, 0)
    ospec = pl.BlockSpec((tm, DSWA_WIDTH), row)
    return pl.pallas_call(
        _proj_dswa_kernel,
        grid=(m // tm,),
        in_specs=[pl.BlockSpec((tm, D_MODEL), row),
                  pl.BlockSpec(w.shape, lambda i: (0, 0), pipeline_mode=pl.Buffered(1))],
        out_specs=[ospec] * 5,
        out_shape=[jax.ShapeDtypeStruct((m, DSWA_WIDTH), BF16),
                   jax.ShapeDtypeStruct((m, DSWA_WIDTH), F32),
                   jax.ShapeDtypeStruct((m, DSWA_WIDTH), F32),
                   jax.ShapeDtypeStruct((m, DSWA_WIDTH), BF16),
                   jax.ShapeDtypeStruct((m, DSWA_WIDTH), BF16)],
        compiler_params=_params(("parallel",)),
        name="proj_dswa",
    )(x, w)


def _gla_constants():
    c = GLA_CHUNK
    t = np.arange(c)
    tril = (t[None, :] <= t[:, None]).astype(np.float32)
    rows = [t, np.full(c, 31), 32 * (t // 32) + 15, 16 * (t // 16) + 7, np.full(c, c - 1)]
    cum = np.concatenate([tril[r] for r in rows], axis=0)

    def level_mask(half):
        blk = 2 * half
        return ((t[:, None] // blk == t[None, :] // blk) & (t[:, None] % blk >= half)
                & (t[None, :] % blk < half)).astype(np.float32)

    lmask = np.stack([level_mask(32), level_mask(16), level_mask(8)])
    dmask = np.stack([((t[None, :] == 8 * (t[:, None] // 8) + s) & (t[:, None] % 8 >= s)).astype(np.float32)
                      for s in range(8)])
    return jnp.asarray(cum, BF16), jnp.asarray(lmask), jnp.asarray(dmask)


def _gla_kernel(q_ref, k_ref, g_ref, v_ref, r_ref, nw_ref, s0_ref, cum_ref, lmask_ref, dmask_ref,
                o_ref, sout_ref, s_sc, *, n_chunks, n_steps, heads):
    step = pl.program_id(2)
    c = GLA_CHUNK

    @pl.when(step == 0)
    def _():
        s_sc[...] = s0_ref[0]

    for ci, hh in [(ci, hh) for ci in range(n_chunks) for hh in range(heads)]:
        rows = slice(ci * c, (ci + 1) * c)
        kcols = slice(hh * GLA_DK, (hh + 1) * GLA_DK)
        vcols = slice(hh * GLA_DV, (hh + 1) * GLA_DV)
        q = q_ref[rows, kcols] * (GLA_DK ** -0.5)
        k = k_ref[rows, kcols]
        g = g_ref[rows, kcols]
        v = v_ref[rows, vcols]
        g1 = g.astype(BF16)
        e1 = g - g1.astype(F32)
        g2 = e1.astype(BF16)
        g3 = (e1 - g2.astype(F32)).astype(BF16)
        cb = _dot(cum_ref[...], jnp.concatenate([g1, g2, g3], axis=1))
        cb = cb[:, 0:GLA_DK] + cb[:, GLA_DK:2 * GLA_DK] + cb[:, 2 * GLA_DK:3 * GLA_DK]
        b = cb[0:c]
        b_end = cb[4 * c:5 * c]
        state = s_sc[hh]
        o = _dot((q * jnp.exp(b)).astype(BF16), state.astype(BF16))
        att = jnp.zeros((c, c), F32)
        for li in range(3):
            bl = cb[(li + 1) * c:(li + 2) * c]
            ql = (q * jnp.exp(jnp.minimum(b - bl, 0.0))).astype(BF16)
            kl = (k * jnp.exp(jnp.minimum(bl - b, 0.0))).astype(BF16)
            att = att + _dot_nt(ql, kl) * lmask_ref[li]
        k3 = k.reshape(c // 8, 8, GLA_DK)
        b3 = b.reshape(c // 8, 8, GLA_DK)
        for sl in range(8):
            kb = jnp.broadcast_to(k3[:, sl:sl + 1, :], k3.shape).reshape(c, GLA_DK)
            bb = jnp.broadcast_to(b3[:, sl:sl + 1, :], b3.shape).reshape(c, GLA_DK)
            d = q * kb * jnp.exp(jnp.minimum(b - bb, 0.0))
            att = att + jnp.sum(d, axis=-1, keepdims=True) * dmask_ref[sl]
        o = o + _dot(att.astype(BF16), v)
        kd_t = (k * jnp.exp(b_end - b)).T.astype(BF16)
        e_col = jnp.exp(b_end).T[:, 0:1]
        s_sc[hh] = state * e_col + _dot(kd_t, v)
        on = o * lax.rsqrt(jnp.mean(o * o, axis=-1, keepdims=True) + RMS_EPS) * nw_ref[:, vcols]
        r = r_ref[rows, vcols]
        o_ref[rows, vcols] = (on * (r * jax.nn.sigmoid(r))).astype(o_ref.dtype)

    @pl.when(step == n_steps - 1)
    def _():
        sout_ref[0] = s_sc[...]


def _gla(q, k, g, v, r, nw, s0, batch, length, rows_per_step, heads):
    n_steps = length // rows_per_step
    cum, lmask, dmask = _gla_constants()
    qk_spec = pl.BlockSpec((rows_per_step, heads * GLA_DK), lambda b, h, t: (b * n_steps + t, h))
    v_spec = pl.BlockSpec((rows_per_step, heads * GLA_DV), lambda b, h, t: (b * n_steps + t, h))
    s_spec = pl.BlockSpec((1, heads, GLA_DK, GLA_DV), lambda b, h, t: (b, h, 0, 0))
    const2 = lambda b, h, t: (0, 0)
    const3 = lambda b, h, t: (0, 0, 0)
    kern = functools.partial(_gla_kernel, n_chunks=rows_per_step // GLA_CHUNK, n_steps=n_steps, heads=heads)
    return pl.pallas_call(
        kern,
        grid=(batch, GLA_HEADS // heads, n_steps),
        in_specs=[qk_spec, qk_spec, qk_spec, v_spec, v_spec,
                  pl.BlockSpec((1, heads * GLA_DV), lambda b, h, t: (0, h)),
                  s_spec,
                  pl.BlockSpec(cum.shape, const2),
                  pl.BlockSpec(lmask.shape, const3),
                  pl.BlockSpec(dmask.shape, const3)],
        out_specs=[v_spec, s_spec],
        out_shape=[jax.ShapeDtypeStruct((batch * length, GLA_WIDTH), BF16),
                   jax.ShapeDtypeStruct((batch, GLA_HEADS, GLA_DK, GLA_DV), F32)],
        scratch_shapes=[pltpu.VMEM((heads, GLA_DK, GLA_DV), F32)],
        compiler_params=_params(("parallel", "parallel", "arbitrary")),
        name="gla",
    )(q, k, g, v, r, nw, s0, cum, lmask, dmask)


def _alibi_slopes():
    return 2.0 ** (-8.0 * (np.arange(DSWA_HEADS, dtype=np.float64) + 1.0) / DSWA_HEADS)


def _branch_bias(dist):
    dist = np.asarray(dist, np.int64)
    mult = np.zeros(dist.shape, np.float64)
    for window, dil in DSWA_BRANCHES:
        mult += (dist >= 0) & (dist <= window) & (dist % dil == 0)
    slopes = _alibi_slopes().reshape((-1,) + (1,) * dist.ndim)
    bias = np.where(mult > 0, np.log(np.maximum(mult, 1.0)) - slopes * dist, NEG_BIG)
    return bias.astype(np.float32)


def _dswa_kernel(q_ref, k_ref, v_ref, bias_ref, o_ref, m_sc, l_sc, acc_sc, *, blk, heads):
    qi = pl.program_id(2)
    e = DSWA_HEAD_DIM
    m_sc[...] = jnp.full(m_sc.shape, NEG_BIG, F32)
    l_sc[...] = jnp.zeros(l_sc.shape, F32)
    acc_sc[...] = jnp.zeros(acc_sc.shape, F32)

    def body(kj, carry):
        off = pl.multiple_of(kj * blk, blk)
        for g in range(heads):
            cols = slice(g * e, (g + 1) * e)
            kb = k_ref[0, pl.ds(off, blk), cols]
            vb = v_ref[0, pl.ds(off, blk), cols]
            s = _dot_nt(q_ref[:, cols], kb) + bias_ref[g, qi - kj]
            m_prev = m_sc[g]
            m_new = jnp.maximum(m_prev, jnp.max(s, axis=-1, keepdims=True))
            p = jnp.exp(s - jnp.concatenate([m_new] * (blk // LANES), axis=1))
            alpha = jnp.exp(m_prev - m_new)
            l_sc[g] = alpha * l_sc[g] + jnp.sum(p, axis=-1, keepdims=True)
            acc_sc[g] = alpha * acc_sc[g] + _dot(p.astype(BF16), vb)
            m_sc[g] = m_new
        return carry

    lax.fori_loop(0, qi + 1, body, 0)
    for g in range(heads):
        o_ref[:, g * e:(g + 1) * e] = (acc_sc[g] / l_sc[g]).astype(o_ref.dtype)


def _dswa_prompt(qb, kb, vb, batch, seq, blk, heads):
    nq = seq // blk
    t = np.arange(blk)
    dist = (np.arange(nq)[:, None, None] * blk + t[None, :, None] - t[None, None, :])
    bias = jnp.asarray(_branch_bias(dist))
    k3 = kb.reshape(batch, seq, DSWA_WIDTH)
    v3 = vb.reshape(batch, seq, DSWA_WIDTH)
    width = heads * DSWA_HEAD_DIM
    q_spec = pl.BlockSpec((blk, width), lambda h, b, i: (b * nq + i, h))
    kv_spec = pl.BlockSpec((1, seq, width), lambda h, b, i: (b, 0, h))
    stat = pltpu.VMEM((heads, blk, DSWA_HEAD_DIM), F32)
    return pl.pallas_call(
        functools.partial(_dswa_kernel, blk=blk, heads=heads),
        grid=(DSWA_HEADS // heads, batch, nq),
        in_specs=[q_spec, kv_spec, kv_spec,
                  pl.BlockSpec((heads, nq, blk, blk), lambda h, b, i: (h, 0, 0, 0),
                               pipeline_mode=pl.Buffered(1))],
        out_specs=q_spec,
        out_shape=jax.ShapeDtypeStruct((batch * seq, DSWA_WIDTH), BF16),
        scratch_shapes=[stat, stat, stat],
        compiler_params=_params(("parallel", "parallel", "arbitrary")),
        name="dswa_prompt",
    )(qb, k3, v3, bias)


def _dswa_sample_kernel(q_ref, ka_ref, va_ref, kb_ref, vb_ref, kn_ref, vn_ref, ba_ref, bb_ref, bn_ref,
                        o_ref):
    e = DSWA_HEAD_DIM
    q = q_ref[0]
    flat = lambda ref_val: ref_val.reshape(-1, e).astype(BF16)
    sa = _dot_nt(q, flat(ka_ref[0, 0])) + ba_ref[...]
    sb = _dot_nt(q, flat(kb_ref[0])) + bb_ref[...]
    sn = _dot_nt(q, kn_ref[0]) + bn_ref[...]
    rowmax = lambda s: jnp.max(s, axis=-1, keepdims=True)
    m = jnp.maximum(jnp.maximum(rowmax(sa), rowmax(sb)), rowmax(sn))
    pa = jnp.exp(sa - m)
    pb = jnp.exp(sb - m)
    pn = jnp.exp(sn - m)
    rowsum = lambda p: jnp.sum(p, axis=-1, keepdims=True)
    den = rowsum(pa) + rowsum(pb) + rowsum(pn)
    out = (_dot(pa.astype(BF16), flat(va_ref[0, 0])) + _dot(pb.astype(BF16), flat(vb_ref[0]))
           + _dot(pn.astype(BF16), vn_ref[0]))
    o_ref[0] = out / den


def _cross_head_bias(bias):
    h, n_new, n_keys = bias.shape
    full = np.full((h, n_new, n_keys, h), NEG_BIG, np.float32)
    for hh in range(h):
        full[hh, :, :, hh] = bias[hh]
    return full.reshape(h * n_new, n_keys * h)


def _dswa_sample(qb, k_new, v_new, cache_k, cache_v):
    batch, n_new, h, e = qb.shape
    n_past = cache_k.shape[1]
    tail = DSWA_BRANCHES[1][0]
    wide_window, wide_dil = DSWA_BRANCHES[2]
    assert DSWA_BRANCHES[0][0] <= tail
    assert n_past % tail == 0 and n_past % wide_dil == 0 and tail % wide_dil == 0 and n_new <= wide_dil
    n_groups = (n_past - tail) // wide_dil
    rows = h * n_new
    lq = np.arange(n_new)
    pos_a = n_past - tail + np.arange(tail)
    pos_b = (wide_dil * np.arange(n_groups)[:, None] + np.arange(n_new)[None, :]).reshape(-1)
    bias_a = _cross_head_bias(_branch_bias(n_past + lq[:, None] - pos_a[None, :]))
    bias_b = _cross_head_bias(_branch_bias(n_past + lq[:, None] - pos_b[None, :]))
    bias_n = np.full((rows, LANES), NEG_BIG, np.float32)
    bias_n[:, :n_new * h] = _cross_head_bias(_branch_bias(lq[:, None] - lq[None, :]))
    q2 = qb.transpose(0, 2, 1, 3).reshape(batch, rows, e)
    pad = ((0, 0), (0, LANES - n_new * h), (0, 0))
    kn = jnp.pad(k_new.reshape(batch, n_new * h, e).astype(BF16), pad)
    vn = jnp.pad(v_new.reshape(batch, n_new * h, e).astype(BF16), pad)
    tail_view = lambda c: c.reshape(batch, n_past // tail, tail, h, e)
    wide_view = lambda c: c.reshape(batch, n_past // wide_dil, wide_dil, h, e)
    tail_spec = pl.BlockSpec((1, 1, tail, h, e), lambda b: (b, n_past // tail - 1, 0, 0, 0))
    wide_spec = pl.BlockSpec((1, n_groups, n_new, h, e), lambda b: (b, 0, 0, 0, 0))
    b3 = lambda b: (b, 0, 0)
    c2 = lambda b: (0, 0)
    out = pl.pallas_call(
        _dswa_sample_kernel,
        grid=(batch,),
        in_specs=[pl.BlockSpec((1, rows, e), b3), tail_spec, tail_spec, wide_spec, wide_spec,
                  pl.BlockSpec((1, LANES, e), b3), pl.BlockSpec((1, LANES, e), b3),
                  pl.BlockSpec(bias_a.shape, c2), pl.BlockSpec(bias_b.shape, c2),
                  pl.BlockSpec(bias_n.shape, c2)],
        out_specs=pl.BlockSpec((1, rows, e), b3),
        out_shape=jax.ShapeDtypeStruct((batch, rows, e), F32),
        compiler_params=_params(("parallel",)),
        name="dswa_sample",
    )(q2, tail_view(cache_k), tail_view(cache_v), wide_view(cache_k), wide_view(cache_v), kn, vn,
      jnp.asarray(bias_a), jnp.asarray(bias_b), jnp.asarray(bias_n))
    return out.reshape(batch, h, n_new, e).transpose(0, 2, 1, 3).reshape(batch, n_new, h * e)


def _layer_norm(pre, g, b):
    mu = jnp.mean(pre, axis=-1, keepdims=True)
    cen = pre - mu
    var = jnp.mean(cen * cen, axis=-1, keepdims=True)
    return cen * lax.rsqrt(var + LN_EPS) * g + b


def _outproj_kernel(ogp_ref, odp_ref, xp_ref, ogs_ref, ods_ref, xs_ref, wo_ref, g_ref, b_ref,
                    wr_ref, br_ref, tri_ref, h_ref, route_ref, cnt_ref, cnt_sc, *, n_prompt_tiles):
    i = pl.program_id(0)
    is_s = i >= n_prompt_tiles

    @pl.when(i == 0)
    def _():
        cnt_sc[...] = jnp.zeros(cnt_sc.shape, F32)

    og = jnp.where(is_s, ogs_ref[...], ogp_ref[...])
    od = jnp.where(is_s, ods_ref[...], odp_ref[...])
    x = jnp.where(is_s, xs_ref[...], xp_ref[...])
    acc = _dot(og, wo_ref[0:GLA_WIDTH, :]) + _dot(od, wo_ref[GLA_WIDTH:, :])
    h = _layer_norm(DEEPNORM_ALPHA * x + acc, g_ref[...], b_ref[...])
    h_ref[...] = h
    logits = _dot(h.astype(BF16), wr_ref[...]) + br_ref[...]
    lane = lax.broadcasted_iota(I32, logits.shape, 1).astype(F32)
    tops, sels, idxs = [], [], []
    cur = logits
    for _ in range(TOP_K):
        mx = jnp.max(cur, axis=-1, keepdims=True)
        idx = jnp.min(jnp.where(cur == mx, lane, float(LANES)), axis=-1, keepdims=True)
        sel = lane == idx
        tops.append(mx)
        sels.append(sel)
        idxs.append(idx)
        cur = jnp.where(sel, -jnp.inf, cur)
    exps = [jnp.exp(t - tops[0]) for t in tops]
    den = exps[0] + exps[1] + exps[2] + exps[3]
    multi = jnp.zeros(logits.shape, F32)
    for sel in sels:
        multi = multi + jnp.where(sel, 1.0, 0.0)
    rank_all = _dot(tri_ref[...], multi.astype(BF16)) + cnt_sc[...]
    cnt_sc[...] = cnt_sc[...] + jnp.sum(multi, axis=0, keepdims=True)
    cnt_ref[...] = cnt_sc[...]
    route = jnp.zeros(logits.shape, F32)
    for kk in range(TOP_K):
        rank = jnp.sum(jnp.where(sels[kk], rank_all, 0.0), axis=-1, keepdims=True)
        route = jnp.where(lane == kk, idxs[kk], route)
        route = jnp.where(lane == TOP_K + kk, exps[kk] / den, route)
        route = jnp.where(lane == 2 * TOP_K + kk, rank, route)
    route_ref[...] = route


def _outproj(og_p, od_p, x_p, og_s, od_s, x_s, wo, g, b, wr, br, tm):
    n_p = x_p.shape[0] // tm
    tri = jnp.asarray(np.tril(np.ones((tm, tm), np.float32), -1), BF16)
    prow = lambda i: (jnp.minimum(i, n_p - 1), 0)
    c2 = lambda i: (0, 0)
    row = lambda i: (i, 0)
    m_tot = x_p.shape[0] + tm
    return pl.pallas_call(
        functools.partial(_outproj_kernel, n_prompt_tiles=n_p),
        grid=(n_p + 1,),
        in_specs=[pl.BlockSpec((tm, GLA_WIDTH), prow), pl.BlockSpec((tm, DSWA_WIDTH), prow),
                  pl.BlockSpec((tm, D_MODEL), prow),
                  pl.BlockSpec((tm, GLA_WIDTH), c2), pl.BlockSpec((tm, DSWA_WIDTH), c2),
                  pl.BlockSpec((tm, D_MODEL), c2),
                  pl.BlockSpec(wo.shape, c2), pl.BlockSpec(g.shape, c2), pl.BlockSpec(b.shape, c2),
                  pl.BlockSpec(wr.shape, c2), pl.BlockSpec(br.shape, c2), pl.BlockSpec(tri.shape, c2)],
        out_specs=[pl.BlockSpec((tm, D_MODEL), row), pl.BlockSpec((tm, LANES), row),
                   pl.BlockSpec((1, LANES), c2)],
        out_shape=[jax.ShapeDtypeStruct((m_tot, D_MODEL), F32),
                   jax.ShapeDtypeStruct((m_tot, LANES), F32),
                   jax.ShapeDtypeStruct((1, LANES), F32)],
        scratch_shapes=[pltpu.VMEM((1, LANES), F32)],
        compiler_params=_params(("arbitrary",)),
        name="outproj_ln_router",
    )(og_p, od_p, x_p, og_s, od_s, x_s, wo, g, b, wr, br, tri)


def _dispatch_kernel(cnt_ref, pst_ref, dest_ref, h_hbm, xr_ref, hbuf, zero_sc, lsems, sems, zsem, *,
                     tm, tb, n_blocks, n_steps):
    i = pl.program_id(0)
    slot = i % 2
    ring = i % 3

    def tile_load(step):
        rows = pl.ds(pl.multiple_of(step * tm, tm), tm)
        return pltpu.make_async_copy(h_hbm.at[rows, :], hbuf.at[step % 3, :, 0, :], lsems.at[step % 3])

    @pl.when(i == 0)
    def _():
        tile_load(0).start()

    @pl.when(i + 1 < n_steps)
    def _():
        tile_load(i + 1).start()

    tile_load(i).wait()
    h_ref = hbuf.at[ring]

    def row_copy(t, r, s):
        return pltpu.make_async_copy(h_ref.at[t], xr_ref.at[r], sems.at[s])

    def issue(t, carry):
        for kk in range(TOP_K):
            row_copy(t, dest_ref[0, 0, t * TOP_K + kk], slot).start(priority=kk % 2)
        return carry

    lax.fori_loop(0, tm, issue, 0)

    def zero_copy(r):
        return pltpu.make_async_copy(zero_sc.at[0], xr_ref.at[r], zsem)

    @pl.when(i == 0)
    def _():
        zero_sc[...] = jnp.zeros(zero_sc.shape, F32)

        def per_expert(e, carry):
            cnt = cnt_ref[e]
            base = pst_ref[e]
            end = (cnt + tb - 1) // tb * tb

            def start(r, c2):
                zero_copy(base + r).start()
                return c2

            def wait(r, c2):
                zero_copy(base + r).wait()
                return c2

            lax.fori_loop(cnt, end, start, 0)
            lax.fori_loop(cnt, end, wait, 0)
            return carry

        lax.fori_loop(0, N_EXPERTS, per_expert, 0)

        last = N_EXPERTS - 1
        n_used = (pst_ref[last] + (cnt_ref[last] + tb - 1) // tb * tb) // tb

        def tail_copy(blk):
            return pltpu.make_async_copy(zero_sc, xr_ref.at[pl.ds(pl.multiple_of(blk * tb, tb), tb)], zsem)

        def tail_start(blk, c2):
            tail_copy(blk).start()
            return c2

        def tail_wait(blk, c2):
            tail_copy(blk).wait()
            return c2

        lax.fori_loop(n_used, n_blocks, tail_start, 0)
        lax.fori_loop(n_used, n_blocks, tail_wait, 0)

    def drain(s):
        def body(t, carry):
            for kk in range(TOP_K):
                row_copy(t, 0, s).wait()
            return carry
        lax.fori_loop(0, tm, body, 0)

    @pl.when(i > 0)
    def _():
        drain(1 - slot)

    @pl.when(i == n_steps - 1)
    def _():
        drain(slot)


def _dispatch(h, dest, counts, pstart, n_rows, tm, tb):
    m = h.shape[0]
    n_t = m // tm
    dest3 = dest.reshape(n_t, 1, tm * TOP_K)
    grid_spec = pltpu.PrefetchScalarGridSpec(
        num_scalar_prefetch=2,
        grid=(n_t,),
        in_specs=[pl.BlockSpec((1, 1, tm * TOP_K), lambda i, c, p: (i, 0, 0), memory_space=pltpu.SMEM),
                  pl.BlockSpec(memory_space=pl.ANY)],
        out_specs=pl.BlockSpec(memory_space=pl.ANY),
        scratch_shapes=[pltpu.VMEM((3, tm, 1, D_MODEL), F32), pltpu.VMEM((tb, 1, D_MODEL), F32),
                        pltpu.SemaphoreType.DMA((3,)), pltpu.SemaphoreType.DMA((2,)),
                        pltpu.SemaphoreType.DMA(())],
    )
    return pl.pallas_call(
        functools.partial(_dispatch_kernel, tm=tm, tb=tb, n_blocks=n_rows // tb, n_steps=n_t),
        grid_spec=grid_spec,
        out_shape=jax.ShapeDtypeStruct((n_rows, 1, D_MODEL), F32),
        compiler_params=_params(("arbitrary",)),
        name="moe_dispatch",
    )(counts, pstart, dest3, h)


def _staged_expert_kernel(be_ref, nvb_ref, lo_ref, hi_ref, kick_ref, nxt_ref, par_ref, x_ref, *rest,
                          n_mats, compute):
    w_refs = rest[:n_mats]
    b_refs = rest[n_mats:2 * n_mats]
    o_ref = rest[2 * n_mats]
    wb, stage, sems, dense = rest[2 * n_mats + 1:]
    i = pl.program_id(0)
    n_chunks = wb.shape[2] // MOE_W_CHUNK

    def chunk_rows(c):
        return pl.ds(pl.multiple_of(c * MOE_W_CHUNK, MOE_W_CHUNK), MOE_W_CHUNK)

    def copies(e, c):
        slot = c % 2
        return [pltpu.make_async_copy(w.at[e, chunk_rows(c), :], stage.at[slot, mi], sems.at[slot])
                for mi, w in enumerate(w_refs)]

    def start(e, c):
        for cp in copies(e, c):
            cp.start()

    def finish_range(e, lo, hi, dst):
        def body(c, carry):
            for cp in copies(e, c):
                cp.wait()
            for mi in range(n_mats):
                wb[dst, mi, chunk_rows(c), :] = stage[c % 2, mi].astype(BF16)

            @pl.when(c + 2 < n_chunks)
            def _():
                start(e, c + 2)

            return carry

        lax.fori_loop(lo, hi, body, 0)

    cur = par_ref[i]

    @pl.when(i == 0)
    def _():
        start(be_ref[0], 0)
        start(be_ref[0], 1)
        finish_range(be_ref[0], 0, n_chunks, cur)

    @pl.when(kick_ref[i] == 1)
    def _():
        start(nxt_ref[i], 0)
        start(nxt_ref[i], 1)

    @pl.when(i < nvb_ref[0])
    def _():
        compute(x_ref, [wb.at[cur, mi] for mi in range(n_mats)], b_refs, o_ref, dense)

    @pl.when(i >= nvb_ref[0])
    def _():
        o_ref[...] = jnp.zeros(o_ref.shape, o_ref.dtype)

    finish_range(nxt_ref[i], lo_ref[i], hi_ref[i], 1 - cur)


def _up_compute(x_ref, w, b_refs, a_ref, dense):
    dense[...] = x_ref[:, 0, :]
    xb = dense[...].astype(BF16)
    half = a_ref.shape[1] // 2
    for n in range(2):
        cols = slice(n * half, (n + 1) * half)
        gate = jnp.minimum(_dot(xb, w[0][:, cols]) + b_refs[0][0, :, cols], SWIGLU_LIMIT)
        up = jnp.clip(_dot(xb, w[1][:, cols]) + b_refs[1][0, :, cols], -SWIGLU_LIMIT, SWIGLU_LIMIT)
        a_ref[:, cols] = (gate * jax.nn.sigmoid(SWIGLU_ALPHA * gate) * (up + 1.0)).astype(a_ref.dtype)


def _down_compute(a_ref, w, b_refs, y_ref, dense):
    ab = a_ref[...]
    half = y_ref.shape[2] // 2
    for n in range(2):
        cols = slice(n * half, (n + 1) * half)
        y_ref[:, 0, cols] = _dot(ab, w[0][:, cols]) + b_refs[0][0, :, cols]


def _staged_expert_call(x_rows, meta, weights, biases, out_dtype, compute, name, tb, row_major_out=False):
    n_rows = x_rows.shape[0]
    k, n_out = weights[0].shape[1:]
    n_mats = len(weights)
    x_block = (tb,) + x_rows.shape[1:]
    rows_map = lambda i, be, nv, *_: (jnp.minimum(i, nv[0] - 1),) + (0,) * (len(x_block) - 1)
    bias_spec = pl.BlockSpec((1, 1, n_out), lambda i, be, *_: (be[i], 0, 0))
    if row_major_out:
        out_shape, out_spec = (n_rows, 1, n_out), pl.BlockSpec((tb, 1, n_out), lambda i, *_: (i, 0, 0))
    else:
        out_shape, out_spec = (n_rows, n_out), pl.BlockSpec((tb, n_out), lambda i, *_: (i, 0))
    grid_spec = pltpu.PrefetchScalarGridSpec(
        num_scalar_prefetch=len(meta),
        grid=(n_rows // tb,),
        in_specs=([pl.BlockSpec(x_block, rows_map)] + [pl.BlockSpec(memory_space=pl.ANY)] * n_mats
                  + [bias_spec] * n_mats),
        out_specs=out_spec,
        scratch_shapes=[pltpu.VMEM((2, n_mats, k, n_out), BF16),
                        pltpu.VMEM((2, n_mats, MOE_W_CHUNK, n_out), F32),
                        pltpu.SemaphoreType.DMA((2,)),
                        pltpu.VMEM((tb, k) if len(x_block) == 3 else (8, LANES), F32)],
    )
    return pl.pallas_call(
        functools.partial(_staged_expert_kernel, n_mats=n_mats, compute=compute),
        grid_spec=grid_spec,
        out_shape=jax.ShapeDtypeStruct(out_shape, out_dtype),
        compiler_params=_params(("arbitrary",)),
        name=name,
    )(*meta, x_rows, *weights, *[b.reshape(N_EXPERTS, 1, n_out) for b in biases])


def _expert_schedule(blk_e, nvb, padded, pstart, tb, n_chunks):
    nb = blk_e.shape[0]
    ids = jnp.arange(N_EXPERTS, dtype=I32)
    n_blk = padded // tb
    has = n_blk > 0
    later = jnp.where(has[None, :] & (ids[None, :] > ids[:, None]), ids[None, :], N_EXPERTS)
    nxt_e = jnp.min(later, axis=1)
    nxt_e = jnp.where(nxt_e == N_EXPERTS, -1, nxt_e)
    par_e = jnp.maximum(jnp.cumsum(has.astype(I32)) - 1, 0) % 2
    bi = jnp.arange(nb, dtype=I32)
    rank = bi - (pstart // tb)[blk_e]
    n_e = jnp.maximum(n_blk[blk_e], 1)
    active = (bi < nvb[0]) & (nxt_e[blk_e] >= 0)
    lo = jnp.where(active, n_chunks * rank // n_e, 0)
    hi = jnp.where(active, n_chunks * (rank + 1) // n_e, 0)
    kick = (active & (rank == 0)).astype(I32)
    to_i32 = lambda a: a.astype(I32)
    return tuple(map(to_i32, (blk_e, nvb, lo, hi, kick, nxt_e[blk_e], par_e[blk_e])))


def _combine_kernel(dcur_ref, dnxt_ref, y_ref, h_ref, gate_ref, g_ref, b_ref, yp_ref, ys_ref,
                    buf, sems, *, tm, n_tiles, n_prompt_tiles):
    i = pl.program_id(0)
    slot = i % 2

    def row_copy(dref, t, kk, sl):
        r = dref[0, 0, t * TOP_K + kk]
        return pltpu.make_async_copy(y_ref.at[pl.ds(r, 1)], buf.at[sl, kk, pl.ds(t, 1)], sems.at[sl])

    def issue(dref, sl):
        def body(t, carry):
            for kk in range(TOP_K):
                row_copy(dref, t, kk, sl).start(priority=kk % 2)
            return carry
        lax.fori_loop(0, tm, body, 0)

    @pl.when(i == 0)
    def _():
        issue(dcur_ref, 0)

    @pl.when(i + 1 < n_tiles)
    def _():
        issue(dnxt_ref, 1 - slot)

    def drain(t, carry):
        for kk in range(TOP_K):
            row_copy(dcur_ref, t, kk, slot).wait()
        return carry

    lax.fori_loop(0, tm, drain, 0)
    gate = gate_ref[...]
    f = jnp.zeros((tm, D_MODEL), F32)
    for kk in range(TOP_K):
        f = f + buf[slot, kk, :, 0, :] * gate[:, TOP_K + kk:TOP_K + kk + 1]
    y = _layer_norm(DEEPNORM_ALPHA * h_ref[...] + f, g_ref[...], b_ref[...])

    @pl.when(i < n_prompt_tiles)
    def _():
        yp_ref[...] = y

    @pl.when(i >= n_prompt_tiles)
    def _():
        ys_ref[...] = y


def _combine(y_rows, h, route, dest, g, b, n_prompt, n_sample, tm):
    n_p = n_prompt // tm
    n_tiles = n_p + n_sample // tm
    dest3 = dest.reshape(-1, 1, tm * TOP_K)
    smem = functools.partial(pl.BlockSpec, (1, 1, tm * TOP_K), memory_space=pltpu.SMEM)
    c2 = lambda i: (0, 0)
    return pl.pallas_call(
        functools.partial(_combine_kernel, tm=tm, n_tiles=n_tiles, n_prompt_tiles=n_p),
        grid=(n_tiles,),
        in_specs=[smem(lambda i: (i, 0, 0)),
                  smem(lambda i: (jnp.minimum(i + 1, n_tiles - 1), 0, 0)),
                  pl.BlockSpec(memory_space=pl.ANY),
                  pl.BlockSpec((tm, D_MODEL), lambda i: (i, 0)),
                  pl.BlockSpec((tm, LANES), lambda i: (i, 0)),
                  pl.BlockSpec(g.shape, c2), pl.BlockSpec(b.shape, c2)],
        out_specs=[pl.BlockSpec((tm, D_MODEL), lambda i: (jnp.minimum(i, n_p - 1), 0)),
                   pl.BlockSpec((tm, D_MODEL), lambda i: (jnp.maximum(i - n_p, 0), 0))],
        out_shape=[jax.ShapeDtypeStruct((n_prompt, D_MODEL), F32),
                   jax.ShapeDtypeStruct((n_sample, D_MODEL), F32)],
        scratch_shapes=[pltpu.VMEM((2, TOP_K, tm, 1, D_MODEL), F32), pltpu.SemaphoreType.DMA((2,))],
        compiler_params=_params(("arbitrary",)),
        name="moe_combine_ln",
    )(dest3, dest3, y_rows, h, route, g, b)


def _moe_and_norm(h, route, cnt, n_prompt, n_sample, w_exp_gate, b_exp_gate, w_exp_up, b_exp_up,
                  w_exp_down, b_exp_down, ln2_g, ln2_b, tb, out_tm, comb_tm):
    m_tot = h.shape[0]
    top_e = route[:, 0:TOP_K].astype(I32)
    rank = route[:, 2 * TOP_K:3 * TOP_K].astype(I32)
    counts = cnt[0, 0:N_EXPERTS].astype(I32)
    padded = (counts + tb - 1) // tb * tb
    pend = jnp.cumsum(padded)
    pstart = pend - padded
    dest = (pstart[top_e] + rank).reshape(-1)
    nb = (m_tot * TOP_K + N_EXPERTS * (tb - 1) + tb - 1) // tb
    nvb = (pend[-1:] // tb).astype(I32)
    blk_start = jnp.arange(nb, dtype=I32) * tb
    blk_e = jnp.minimum(jnp.sum((blk_start[:, None] >= pend[None, :]).astype(I32), axis=1), N_EXPERTS - 1)
    meta = _expert_schedule(blk_e, nvb, padded, pstart, tb, D_MODEL // MOE_W_CHUNK)
    x_rows = _dispatch(h, dest, counts, pstart.astype(I32), nb * tb, out_tm, tb)
    a_rows = _staged_expert_call(x_rows, meta, (w_exp_gate, w_exp_up), (b_exp_gate, b_exp_up), BF16,
                                 _up_compute, "moe_up", tb)
    y_rows = _staged_expert_call(a_rows, meta, (w_exp_down,), (b_exp_down,), F32,
                                 _down_compute, "moe_down", tb, row_major_out=True)
    return _combine(y_rows, h, route, dest, ln2_g, ln2_b, n_prompt, n_sample, comb_tm)


def _split_w_in(w_in):
    sizes = (GLA_QK, GLA_QK, GLA_WIDTH, GLA_GATE_RANK, GLA_WIDTH, DSWA_WIDTH, DSWA_WIDTH, DSWA_WIDTH)
    o = np.cumsum((0,) + sizes)
    wg = jnp.concatenate([w_in[:, o[0]:o[3]], w_in[:, o[4]:o[5]]], axis=1).astype(BF16)
    wlr = jnp.pad(w_in[:, o[3]:o[4]], ((0, 0), (0, LANES - GLA_GATE_RANK))).astype(BF16)
    wd = w_in[:, o[5]:o[8]].astype(BF16)
    return wg, wlr, wd


def kernel(x_prompt, x_sample, state_gla, cache_dswa_k, cache_dswa_v, w_in, w_gla_alpha, b_gla_alpha,
           gla_norm_w, w_o, ln1_g, ln1_b, w_router, b_router, w_exp_gate, b_exp_gate, w_exp_up,
           b_exp_up, w_exp_down, b_exp_down, ln2_g, ln2_b):
    bp, sp, _ = x_prompt.shape
    bs, ls, _ = x_sample.shape
    n_prompt, n_sample = bp * sp, bs * ls
    n_past = cache_dswa_k.shape[1]

    wg, wlr, wd = _split_w_in(w_in)
    wa = jnp.pad(w_gla_alpha, ((0, LANES - GLA_GATE_RANK), (0, 0))).astype(BF16)
    ba = b_gla_alpha.reshape(1, GLA_QK)
    nw = gla_norm_w.reshape(1, GLA_WIDTH)
    xp = x_prompt.reshape(n_prompt, D_MODEL)
    xs = x_sample.reshape(n_sample, D_MODEL)

    q, k, v, r, g = _proj_gla(xp, wg, wlr, wa, ba, PROJ_TM)
    og_p, state_p = _gla(q, k, g, v, r, nw, jnp.zeros((bp, GLA_HEADS, GLA_DK, GLA_DV), F32),
                         bp, sp, GLA_ROWS, GLA_STEP_HEADS)
    dqb, dk, dv, dkb, dvb = _proj_dswa(xp, wd, PROJ_TM)
    od_p = _dswa_prompt(dqb, dkb, dvb, bp, sp, ATT_T, ATT_HEADS)

    qs, ks, vs, rs, gs = _proj_gla(xs, wg, wlr, wa, ba, n_sample)
    lpad = GLA_CHUNK - ls
    pad_rows = lambda a: jnp.pad(a.reshape(bs, ls, -1), ((0, 0), (0, lpad), (0, 0))).reshape(bs * GLA_CHUNK, -1)
    og_s_pad, state_s = _gla(pad_rows(qs), pad_rows(ks), pad_rows(gs), pad_rows(vs), pad_rows(rs), nw,
                             state_gla, bs, GLA_CHUNK, GLA_CHUNK, GLA_STEP_HEADS)
    og_s = og_s_pad.reshape(bs, GLA_CHUNK, GLA_WIDTH)[:, :ls].reshape(n_sample, GLA_WIDTH)
    tqb, tk, tv, _, _ = _proj_dswa(xs, wd, n_sample)
    heads4 = lambda a: a.reshape(bs, ls, DSWA_HEADS, DSWA_HEAD_DIM)
    od_s = _dswa_sample(heads4(tqb), heads4(tk), heads4(tv), cache_dswa_k, cache_dswa_v)
    od_s = od_s.reshape(n_sample, DSWA_WIDTH).astype(BF16)

    spad = ((0, OUT_TM - n_sample), (0, 0))
    wr = jnp.pad(w_router, ((0, 0), (0, LANES - N_EXPERTS))).astype(BF16)
    br = jnp.pad(b_router, (0, LANES - N_EXPERTS), constant_values=NEG_BIG).reshape(1, LANES)
    h, route, cnt = _outproj(og_p, od_p, xp, jnp.pad(og_s, spad), jnp.pad(od_s, spad), jnp.pad(xs, spad),
                             w_o.astype(BF16), ln1_g.reshape(1, D_MODEL), ln1_b.reshape(1, D_MODEL),
                             wr, br, OUT_TM)
    y_p, y_s = _moe_and_norm(h, route, cnt, n_prompt, n_sample, w_exp_gate, b_exp_gate, w_exp_up,
                             b_exp_up, w_exp_down, b_exp_down, ln2_g.reshape(1, D_MODEL),
                             ln2_b.reshape(1, D_MODEL), MOE_TB, OUT_TM, COMB_TM)

    return (y_p.reshape(bp, sp, D_MODEL), y_s.reshape(bs, ls, D_MODEL), state_p,
            dk.reshape(bp, sp, DSWA_HEADS, DSWA_HEAD_DIM), dv.reshape(bp, sp, DSWA_HEADS, DSWA_HEAD_DIM),
            state_s, tk.reshape(bs, ls, DSWA_HEADS, DSWA_HEAD_DIM),
            tv.reshape(bs, ls, DSWA_HEADS, DSWA_HEAD_DIM))
```

```python
import functools

import numpy as np
import jax
import jax.numpy as jnp
from jax import lax
from jax.experimental import pallas as pl
from jax.experimental.pallas import tpu as pltpu

F32 = jnp.float32
BF16 = jnp.bfloat16
I32 = jnp.int32

D_MODEL = 2048
GLA_HEADS = 4
GLA_DK = 128
GLA_DV = 256
GLA_WIDTH = GLA_HEADS * GLA_DV
GLA_QK = GLA_HEADS * GLA_DK
GLA_GATE_RANK = 16
GLA_TAU = 16.0
GLA_CHUNK = 64
DSWA_HEADS = 8
DSWA_HEAD_DIM = 128
DSWA_WIDTH = DSWA_HEADS * DSWA_HEAD_DIM
DSWA_BRANCHES = ((128, 1), (512, 4), (2048, 16))
N_EXPERTS = 32
TOP_K = 4
D_FF = D_MODEL
SWIGLU_LIMIT = 7.0
SWIGLU_ALPHA = 1.702
DEEPNORM_ALPHA = 2.0 ** 0.25
LN_EPS = 1e-5
RMS_EPS = 1e-6
NEG_BIG = -1e30

LANES = 128
V7X_VMEM_LIMIT = 56 * 1024 * 1024

PROJ_TM = 512
GLA_ROWS = 256
GLA_STEP_HEADS = 2
ATT_T = 256
ATT_HEADS = 8
OUT_TM = 256
MOE_TB = 256
MOE_W_CHUNK = 128
COMB_TM = 128


def _dot(a, b):
    return jnp.dot(a, b, preferred_element_type=F32)


def _dot_nt(a, b):
    return lax.dot_general(a, b, (((1,), (1,)), ((), ())), preferred_element_type=F32)


def _params(sem, vmem=V7X_VMEM_LIMIT):
    return pltpu.CompilerParams(dimension_semantics=sem, vmem_limit_bytes=vmem)


def _proj_gla_kernel(x_ref, wg_ref, wlr_ref, wa_ref, ba_ref, q_ref, k_ref, v_ref, r_ref, g_ref):
    xb = x_ref[...].astype(BF16)
    q_ref[...] = _dot(xb, wg_ref[:, 0:GLA_QK])
    k_ref[...] = _dot(xb, wg_ref[:, GLA_QK:2 * GLA_QK])
    v_ref[...] = _dot(xb, wg_ref[:, 2 * GLA_QK:2 * GLA_QK + GLA_WIDTH]).astype(v_ref.dtype)
    r_ref[...] = _dot(xb, wg_ref[:, 2 * GLA_QK + GLA_WIDTH:])
    lr = _dot(xb, wlr_ref[...])
    z = _dot(lr.astype(BF16), wa_ref[...]) + ba_ref[...]
    log_sig = jnp.minimum(z, 0.0) - jnp.log1p(jnp.exp(-jnp.abs(z)))
    g_ref[...] = log_sig * (1.0 / GLA_TAU)


def _proj_gla(x, wg, wlr, wa, ba, tm):
    m = x.shape[0]
    const = lambda i: (0, 0)
    row = lambda i: (i, 0)
    return pl.pallas_call(
        _proj_gla_kernel,
        grid=(m // tm,),
        in_specs=[pl.BlockSpec((tm, D_MODEL), row),
                  pl.BlockSpec(wg.shape, const, pipeline_mode=pl.Buffered(1)),
                  pl.BlockSpec(wlr.shape, const),
                  pl.BlockSpec(wa.shape, const),
                  pl.BlockSpec(ba.shape, const)],
        out_specs=[pl.BlockSpec((tm, GLA_QK), row), pl.BlockSpec((tm, GLA_QK), row),
                   pl.BlockSpec((tm, GLA_WIDTH), row), pl.BlockSpec((tm, GLA_WIDTH), row),
                   pl.BlockSpec((tm, GLA_QK), row)],
        out_shape=[jax.ShapeDtypeStruct((m, GLA_QK), F32), jax.ShapeDtypeStruct((m, GLA_QK), F32),
                   jax.ShapeDtypeStruct((m, GLA_WIDTH), BF16), jax.ShapeDtypeStruct((m, GLA_WIDTH), F32),
                   jax.ShapeDtypeStruct((m, GLA_QK), F32)],
        compiler_params=_params(("parallel",)),
        name="proj_gla",
    )(x, wg, wlr, wa, ba)


def _proj_dswa_kernel(x_ref, w_ref, qb_ref, k_ref, v_ref, kb_ref, vb_ref):
    xb = x_ref[...].astype(BF16)
    w = DSWA_WIDTH
    qb_ref[...] = (_dot(xb, w_ref[:, 0:w]) * (DSWA_HEAD_DIM ** -0.5)).astype(BF16)
    k = _dot(xb, w_ref[:, w:2 * w])
    k_ref[...] = k
    kb_ref[...] = k.astype(BF16)
    v = _dot(xb, w_ref[:, 2 * w:3 * w])
    v_ref[...] = v
    vb_ref[...] = v.astype(BF16)


def _proj_dswa(x, w, tm):
    m = x.shape[0]
    row = lambda i: (i, 0)
    ospec = pl.BlockSpec((tm, DSWA_WIDTH), row)
    return pl.pallas_call(
        _proj_dswa_kernel,
        grid=(m // tm,),
        in_specs=[pl.BlockSpec((tm, D_MODEL), row),
                  pl.BlockSpec(w.shape, lambda i: (0, 0), pipeline_mode=pl.Buffered(1))],
        out_specs=[ospec] * 5,
        out_shape=[jax.ShapeDtypeStruct((m, DSWA_WIDTH), BF16),
                   jax.ShapeDtypeStruct((m, DSWA_WIDTH), F32),
                   jax.ShapeDtypeStruct((m, DSWA_WIDTH), F32),
                   jax.ShapeDtypeStruct((m, DSWA_WIDTH), BF16),
                   jax.ShapeDtypeStruct((m, DSWA_WIDTH), BF16)],
        compiler_params=_params(("parallel",)),
        name="proj_dswa",
    )(x, w)


def _gla_constants():
    c = GLA_CHUNK
    t = np.arange(c)
    tril = (t[None, :] <= t[:, None]).astype(np.float32)
    rows = [t, np.full(c, 31), 32 * (t // 32) + 15, 16 * (t // 16) + 7, np.full(c, c - 1)]
    cum = np.concatenate([tril[r] for r in rows], axis=0)

    def level_mask(half):
        blk = 2 * half
        return ((t[:, None] // blk == t[None, :] // blk) & (t[:, None] % blk >= half)
                & (t[None, :] % blk < half)).astype(np.float32)

    lmask = np.stack([level_mask(32), level_mask(16), level_mask(8)])
    dmask = np.stack([((t[None, :] == 8 * (t[:, None] // 8) + s) & (t[:, None] % 8 >= s)).astype(np.float32)
                      for s in range(8)])
    return jnp.asarray(cum, BF16), jnp.asarray(lmask), jnp.asarray(dmask)


def _gla_kernel(q_ref, k_ref, g_ref, v_ref, r_ref, nw_ref, s0_ref, cum_ref, lmask_ref, dmask_ref,
                o_ref, sout_ref, s_sc, *, n_chunks, n_steps, heads):
    step = pl.program_id(2)
    c = GLA_CHUNK

    @pl.when(step == 0)
    def _():
        s_sc[...] = s0_ref[0]

    for ci, hh in [(ci, hh) for ci in range(n_chunks) for hh in range(heads)]:
        rows = slice(ci * c, (ci + 1) * c)
        kcols = slice(hh * GLA_DK, (hh + 1) * GLA_DK)
        vcols = slice(hh * GLA_DV, (hh + 1) * GLA_DV)
        q = q_ref[rows, kcols] * (GLA_DK ** -0.5)
        k = k_ref[rows, kcols]
        g = g_ref[rows, kcols]
        v = v_ref[rows, vcols]
        g1 = g.astype(BF16)
        e1 = g - g1.astype(F32)
        g2 = e1.astype(BF16)
        g3 = (e1 - g2.astype(F32)).astype(BF16)
        cb = _dot(cum_ref[...], jnp.concatenate([g1, g2, g3], axis=1))
        cb = cb[:, 0:GLA_DK] + cb[:, GLA_DK:2 * GLA_DK] + cb[:, 2 * GLA_DK:3 * GLA_DK]
        b = cb[0:c]
        b_end = cb[4 * c:5 * c]
        state = s_sc[hh]
        o = _dot((q * jnp.exp(b)).astype(BF16), state.astype(BF16))
        att = jnp.zeros((c, c), F32)
        for li in range(3):
            bl = cb[(li + 1) * c:(li + 2) * c]
            ql = (q * jnp.exp(jnp.minimum(b - bl, 0.0))).astype(BF16)
            kl = (k * jnp.exp(jnp.minimum(bl - b, 0.0))).astype(BF16)
            att = att + _dot_nt(ql, kl) * lmask_ref[li]
        k3 = k.reshape(c // 8, 8, GLA_DK)
        b3 = b.reshape(c // 8, 8, GLA_DK)
        for sl in range(8):
            kb = jnp.broadcast_to(k3[:, sl:sl + 1, :], k3.shape).reshape(c, GLA_DK)
            bb = jnp.broadcast_to(b3[:, sl:sl + 1, :], b3.shape).reshape(c, GLA_DK)
            d = q * kb * jnp.exp(jnp.minimum(b - bb, 0.0))
            att = att + jnp.sum(d, axis=-1, keepdims=True) * dmask_ref[sl]
        o = o + _dot(att.astype(BF16), v)
        kd_t = (k * jnp.exp(b_end - b)).T.astype(BF16)
        e_col = jnp.exp(b_end).T[:, 0:1]
        s_sc[hh] = state * e_col + _dot(kd_t, v)
        on = o * lax.rsqrt(jnp.mean(o * o, axis=-1, keepdims=True) + RMS_EPS) * nw_ref[:, vcols]
        r = r_ref[rows, vcols]
        o_ref[rows, vcols] = (on * (r * jax.nn.sigmoid(r))).astype(o_ref.dtype)

    @pl.when(step == n_steps - 1)
    def _():
        sout_ref[0] = s_sc[...]


def _gla(q, k, g, v, r, nw, s0, batch, length, rows_per_step, heads):
    n_steps = length // rows_per_step
    cum, lmask, dmask = _gla_constants()
    qk_spec = pl.BlockSpec((rows_per_step, heads * GLA_DK), lambda b, h, t: (b * n_steps + t, h))
    v_spec = pl.BlockSpec((rows_per_step, heads * GLA_DV), lambda b, h, t: (b * n_steps + t, h))
    s_spec = pl.BlockSpec((1, heads, GLA_DK, GLA_DV), lambda b, h, t: (b, h, 0, 0))
    const2 = lambda b, h, t: (0, 0)
    const3 = lambda b, h, t: (0, 0, 0)
    kern = functools.partial(_gla_kernel, n_chunks=rows_per_step // GLA_CHUNK, n_steps=n_steps, heads=heads)
    return pl.pallas_call(
        kern,
        grid=(batch, GLA_HEADS // heads, n_steps),
        in_specs=[qk_spec, qk_spec, qk_spec, v_spec, v_spec,
                  pl.BlockSpec((1, heads * GLA_DV), lambda b, h, t: (0, h)),
                  s_spec,
                  pl.BlockSpec(cum.shape, const2),
                  pl.BlockSpec(lmask.shape, const3),
                  pl.BlockSpec(dmask.shape, const3)],
        out_specs=[v_spec, s_spec],
        out_shape=[jax.ShapeDtypeStruct((batch * length, GLA_WIDTH), BF16),
                   jax.ShapeDtypeStruct((batch, GLA_HEADS, GLA_DK, GLA_DV), F32)],
        scratch_shapes=[pltpu.VMEM((heads, GLA_DK, GLA_DV), F32)],
        compiler_params=_params(("parallel", "parallel", "arbitrary")),
        name="gla",
    )(q, k, g, v, r, nw, s0, cum, lmask, dmask)


def _alibi_slopes():
    return 2.0 ** (-8.0 * (np.arange(DSWA_HEADS, dtype=np.float64) + 1.0) / DSWA_HEADS)


def _branch_bias(dist):
    dist = np.asarray(dist, np.int64)
    mult = np.zeros(dist.shape, np.float64)
    for window, dil in DSWA_BRANCHES:
        mult += (dist >= 0) & (dist <= window) & (dist % dil == 0)
    slopes = _alibi_slopes().reshape((-1,) + (1,) * dist.ndim)
    bias = np.where(mult > 0, np.log(np.maximum(mult, 1.0)) - slopes * dist, NEG_BIG)
    return bias.astype(np.float32)


def _dswa_kernel(q_ref, k_ref, v_ref, bias_ref, o_ref, m_sc, l_sc, acc_sc, *, blk, heads):
    qi = pl.program_id(2)
    e = DSWA_HEAD_DIM
    m_sc[...] = jnp.full(m_sc.shape, NEG_BIG, F32)
    l_sc[...] = jnp.zeros(l_sc.shape, F32)
    acc_sc[...] = jnp.zeros(acc_sc.shape, F32)

    def body(kj, carry):
        off = pl.multiple_of(kj * blk, blk)
        for g in range(heads):
            cols = slice(g * e, (g + 1) * e)
            kb = k_ref[0, pl.ds(off, blk), cols]
            vb = v_ref[0, pl.ds(off, blk), cols]
            s = _dot_nt(q_ref[:, cols], kb) + bias_ref[g, qi - kj]
            m_prev = m_sc[g]
            m_new = jnp.maximum(m_prev, jnp.max(s, axis=-1, keepdims=True))
            p = jnp.exp(s - jnp.concatenate([m_new] * (blk // LANES), axis=1))
            alpha = jnp.exp(m_prev - m_new)
            l_sc[g] = alpha * l_sc[g] + jnp.sum(p, axis=-1, keepdims=True)
            acc_sc[g] = alpha * acc_sc[g] + _dot(p.astype(BF16), vb)
            m_sc[g] = m_new
        return carry

    lax.fori_loop(0, qi + 1, body, 0)
    for g in range(heads):
        o_ref[:, g * e:(g + 1) * e] = (acc_sc[g] / l_sc[g]).astype(o_ref.dtype)


def _dswa_prompt(qb, kb, vb, batch, seq, blk, heads):
    nq = seq // blk
    t = np.arange(blk)
    dist = (np.arange(nq)[:, None, None] * blk + t[None, :, None] - t[None, None, :])
    bias = jnp.asarray(_branch_bias(dist))
    k3 = kb.reshape(batch, seq, DSWA_WIDTH)
    v3 = vb.reshape(batch, seq, DSWA_WIDTH)
    width = heads * DSWA_HEAD_DIM
    q_spec = pl.BlockSpec((blk, width), lambda h, b, i: (b * nq + i, h))
    kv_spec = pl.BlockSpec((1, seq, width), lambda h, b, i: (b, 0, h))
    stat = pltpu.VMEM((heads, blk, DSWA_HEAD_DIM), F32)
    return pl.pallas_call(
        functools.partial(_dswa_kernel, blk=blk, heads=heads),
        grid=(DSWA_HEADS // heads, batch, nq),
        in_specs=[q_spec, kv_spec, kv_spec,
                  pl.BlockSpec((heads, nq, blk, blk), lambda h, b, i: (h, 0, 0, 0),
                               pipeline_mode=pl.Buffered(1))],
        out_specs=q_spec,
        out_shape=jax.ShapeDtypeStruct((batch * seq, DSWA_WIDTH), BF16),
        scratch_shapes=[stat, stat, stat],
        compiler_params=_params(("parallel", "parallel", "arbitrary")),
        name="dswa_prompt",
    )(qb, k3, v3, bias)


def _dswa_sample_kernel(q_ref, ka_ref, va_ref, kb_ref, vb_ref, kn_ref, vn_ref, ba_ref, bb_ref, bn_ref,
                        o_ref):
    e = DSWA_HEAD_DIM
    q = q_ref[0]
    flat = lambda ref_val: ref_val.reshape(-1, e).astype(BF16)
    sa = _dot_nt(q, flat(ka_ref[0, 0])) + ba_ref[...]
    sb = _dot_nt(q, flat(kb_ref[0])) + bb_ref[...]
    sn = _dot_nt(q, kn_ref[0]) + bn_ref[...]
    rowmax = lambda s: jnp.max(s, axis=-1, keepdims=True)
    m = jnp.maximum(jnp.maximum(rowmax(sa), rowmax(sb)), rowmax(sn))
    pa = jnp.exp(sa - m)
    pb = jnp.exp(sb - m)
    pn = jnp.exp(sn - m)
    rowsum = lambda p: jnp.sum(p, axis=-1, keepdims=True)
    den = rowsum(pa) + rowsum(pb) + rowsum(pn)
    out = (_dot(pa.astype(BF16), flat(va_ref[0, 0])) + _dot(pb.astype(BF16), flat(vb_ref[0]))
           + _dot(pn.astype(BF16), vn_ref[0]))
    o_ref[0] = out / den


def _cross_head_bias(bias):
    h, n_new, n_keys = bias.shape
    full = np.full((h, n_new, n_keys, h), NEG_BIG, np.float32)
    for hh in range(h):
        full[hh, :, :, hh] = bias[hh]
    return full.reshape(h * n_new, n_keys * h)


def _dswa_sample(qb, k_new, v_new, cache_k, cache_v):
    batch, n_new, h, e = qb.shape
    n_past = cache_k.shape[1]
    tail = DSWA_BRANCHES[1][0]
    wide_window, wide_dil = DSWA_BRANCHES[2]
    assert DSWA_BRANCHES[0][0] <= tail
    assert n_past % tail == 0 and n_past % wide_dil == 0 and tail % wide_dil == 0 and n_new <= wide_dil
    n_groups = (n_past - tail) // wide_dil
    rows = h * n_new
    lq = np.arange(n_new)
    pos_a = n_past - tail + np.arange(tail)
    pos_b = (wide_dil * np.arange(n_groups)[:, None] + np.arange(n_new)[None, :]).reshape(-1)
    bias_a = _cross_head_bias(_branch_bias(n_past + lq[:, None] - pos_a[None, :]))
    bias_b = _cross_head_bias(_branch_bias(n_past + lq[:, None] - pos_b[None, :]))
    bias_n = np.full((rows, LANES), NEG_BIG, np.float32)
    bias_n[:, :n_new * h] = _cross_head_bias(_branch_bias(lq[:, None] - lq[None, :]))
    q2 = qb.transpose(0, 2, 1, 3).reshape(batch, rows, e)
    pad = ((0, 0), (0, LANES - n_new * h), (0, 0))
    kn = jnp.pad(k_new.reshape(batch, n_new * h, e).astype(BF16), pad)
    vn = jnp.pad(v_new.reshape(batch, n_new * h, e).astype(BF16), pad)
    tail_view = lambda c: c.reshape(batch, n_past // tail, tail, h, e)
    wide_view = lambda c: c.reshape(batch, n_past // wide_dil, wide_dil, h, e)
    tail_spec = pl.BlockSpec((1, 1, tail, h, e), lambda b: (b, n_past // tail - 1, 0, 0, 0))
    wide_spec = pl.BlockSpec((1, n_groups, n_new, h, e), lambda b: (b, 0, 0, 0, 0))
    b3 = lambda b: (b, 0, 0)
    c2 = lambda b: (0, 0)
    out = pl.pallas_call(
        _dswa_sample_kernel,
        grid=(batch,),
        in_specs=[pl.BlockSpec((1, rows, e), b3), tail_spec, tail_spec, wide_spec, wide_spec,
                  pl.BlockSpec((1, LANES, e), b3), pl.BlockSpec((1, LANES, e), b3),
                  pl.BlockSpec(bias_a.shape, c2), pl.BlockSpec(bias_b.shape, c2),
                  pl.BlockSpec(bias_n.shape, c2)],
        out_specs=pl.BlockSpec((1, rows, e), b3),
        out_shape=jax.ShapeDtypeStruct((batch, rows, e), F32),
        compiler_params=_params(("parallel",)),
        name="dswa_sample",
    )(q2, tail_view(cache_k), tail_view(cache_v), wide_view(cache_k), wide_view(cache_v), kn, vn,
      jnp.asarray(bias_a), jnp.asarray(bias_b), jnp.asarray(bias_n))
    return out.reshape(batch, h, n_new, e).transpose(0, 2, 1, 3).reshape(batch, n_new, h * e)


def _layer_norm(pre, g, b):
    mu = jnp.mean(pre, axis=-1, keepdims=True)
    cen = pre - mu
    var = jnp.mean(cen * cen, axis=-1, keepdims=True)
    return cen * lax.rsqrt(var + LN_EPS) * g + b


def _outproj_kernel(ogp_ref, odp_ref, xp_ref, ogs_ref, ods_ref, xs_ref, wo_ref, g_ref, b_ref,
                    wr_ref, br_ref, tri_ref, h_ref, route_ref, cnt_ref, cnt_sc, *, n_prompt_tiles):
    i = pl.program_id(0)
    is_s = i >= n_prompt_tiles

    @pl.when(i == 0)
    def _():
        cnt_sc[...] = jnp.zeros(cnt_sc.shape, F32)

    og = jnp.where(is_s, ogs_ref[...], ogp_ref[...])
    od = jnp.where(is_s, ods_ref[...], odp_ref[...])
    x = jnp.where(is_s, xs_ref[...], xp_ref[...])
    acc = _dot(og, wo_ref[0:GLA_WIDTH, :]) + _dot(od, wo_ref[GLA_WIDTH:, :])
    h = _layer_norm(DEEPNORM_ALPHA * x + acc, g_ref[...], b_ref[...])
    h_ref[...] = h
    logits = _dot(h.astype(BF16), wr_ref[...]) + br_ref[...]
    lane = lax.broadcasted_iota(I32, logits.shape, 1).astype(F32)
    tops, sels, idxs = [], [], []
    cur = logits
    for _ in range(TOP_K):
        mx = jnp.max(cur, axis=-1, keepdims=True)
        idx = jnp.min(jnp.where(cur == mx, lane, float(LANES)), axis=-1, keepdims=True)
        sel = lane == idx
        tops.append(mx)
        sels.append(sel)
        idxs.append(idx)
        cur = jnp.where(sel, -jnp.inf, cur)
    exps = [jnp.exp(t - tops[0]) for t in tops]
    den = exps[0] + exps[1] + exps[2] + exps[3]
    multi = jnp.zeros(logits.shape, F32)
    for sel in sels:
        multi = multi + jnp.where(sel, 1.0, 0.0)
    rank_all = _dot(tri_ref[...], multi.astype(BF16)) + cnt_sc[...]
    cnt_sc[...] = cnt_sc[...] + jnp.sum(multi, axis=0, keepdims=True)
    cnt_ref[...] = cnt_sc[...]
    route = jnp.zeros(logits.shape, F32)
    for kk in range(TOP_K):
        rank = jnp.sum(jnp.where(sels[kk], rank_all, 0.0), axis=-1, keepdims=True)
        route = jnp.where(lane == kk, idxs[kk], route)
        route = jnp.where(lane == TOP_K + kk, exps[kk] / den, route)
        route = jnp.where(lane == 2 * TOP_K + kk, rank, route)
    route_ref[...] = route


def _outproj(og_p, od_p, x_p, og_s, od_s, x_s, wo, g, b, wr, br, tm):
    n_p = x_p.shape[0] // tm
    tri = jnp.asarray(np.tril(np.ones((tm, tm), np.float32), -1), BF16)
    prow = lambda i: (jnp.minimum(i, n_p - 1), 0)
    c2 = lambda i: (0, 0)
    row = lambda i: (i, 0)
    m_tot = x_p.shape[0] + tm
    return pl.pallas_call(
        functools.partial(_outproj_kernel, n_prompt_tiles=n_p),
        grid=(n_p + 1,),
        in_specs=[pl.BlockSpec((tm, GLA_WIDTH), prow), pl.BlockSpec((tm, DSWA_WIDTH), prow),
                  pl.BlockSpec((tm, D_MODEL), prow),
                  pl.BlockSpec((tm, GLA_WIDTH), c2), pl.BlockSpec((tm, DSWA_WIDTH), c2),
                  pl.BlockSpec((tm, D_MODEL), c2),
                  pl.BlockSpec(wo.shape, c2), pl.BlockSpec(g.shape, c2), pl.BlockSpec(b.shape, c2),
                  pl.BlockSpec(wr.shape, c2), pl.BlockSpec(br.shape, c2), pl.BlockSpec(tri.shape, c2)],
        out_specs=[pl.BlockSpec((tm, D_MODEL), row), pl.BlockSpec((tm, LANES), row),
                   pl.BlockSpec((1, LANES), c2)],
        out_shape=[jax.ShapeDtypeStruct((m_tot, D_MODEL), F32),
                   jax.ShapeDtypeStruct((m_tot, LANES), F32),
                   jax.ShapeDtypeStruct((1, LANES), F32)],
        scratch_shapes=[pltpu.VMEM((1, LANES), F32)],
        compiler_params=_params(("arbitrary",)),
        name="outproj_ln_router",
    )(og_p, od_p, x_p, og_s, od_s, x_s, wo, g, b, wr, br, tri)


def _dispatch_kernel(cnt_ref, pst_ref, dest_ref, h_hbm, xr_ref, hbuf, zero_sc, lsems, sems, zsem, *,
                     tm, tb, n_blocks, n_steps):
    i = pl.program_id(0)
    slot = i % 2
    ring = i % 3

    def tile_load(step):
        rows = pl.ds(pl.multiple_of(step * tm, tm), tm)
        return pltpu.make_async_copy(h_hbm.at[rows, :], hbuf.at[step % 3, :, 0, :], lsems.at[step % 3])

    @pl.when(i == 0)
    def _():
        tile_load(0).start()

    @pl.when(i + 1 < n_steps)
    def _():
        tile_load(i + 1).start()

    tile_load(i).wait()
    h_ref = hbuf.at[ring]

    def row_copy(t, r, s):
        return pltpu.make_async_copy(h_ref.at[t], xr_ref.at[r], sems.at[s])

    def issue(t, carry):
        for kk in range(TOP_K):
            row_copy(t, dest_ref[0, 0, t * TOP_K + kk], slot).start(priority=kk % 2)
        return carry

    lax.fori_loop(0, tm, issue, 0)

    def zero_copy(r):
        return pltpu.make_async_copy(zero_sc.at[0], xr_ref.at[r], zsem)

    @pl.when(i == 0)
    def _():
        zero_sc[...] = jnp.zeros(zero_sc.shape, F32)

        def per_expert(e, carry):
            cnt = cnt_ref[e]
            base = pst_ref[e]
            end = (cnt + tb - 1) // tb * tb

            def start(r, c2):
                zero_copy(base + r).start()
                return c2

            def wait(r, c2):
                zero_copy(base + r).wait()
                return c2

            lax.fori_loop(cnt, end, start, 0)
            lax.fori_loop(cnt, end, wait, 0)
            return carry

        lax.fori_loop(0, N_EXPERTS, per_expert, 0)

        last = N_EXPERTS - 1
        n_used = (pst_ref[last] + (cnt_ref[last] + tb - 1) // tb * tb) // tb

        def tail_copy(blk):
            return pltpu.make_async_copy(zero_sc, xr_ref.at[pl.ds(pl.multiple_of(blk * tb, tb), tb)], zsem)

        def tail_start(blk, c2):
            tail_copy(blk).start()
            return c2

        def tail_wait(blk, c2):
            tail_copy(blk).wait()
            return c2

        lax.fori_loop(n_used, n_blocks, tail_start, 0)
        lax.fori_loop(n_used, n_blocks, tail_wait, 0)

    def drain(s):
        def body(t, carry):
            for kk in range(TOP_K):
                row_copy(t, 0, s).wait()
            return carry
        lax.fori_loop(0, tm, body, 0)

    @pl.when(i > 0)
    def _():
        drain(1 - slot)

    @pl.when(i == n_steps - 1)
    def _():
        drain(slot)


def _dispatch(h, dest, counts, pstart, n_rows, tm, tb):
    m = h.shape[0]
    n_t = m // tm
    dest3 = dest.reshape(n_t, 1, tm * TOP_K)
    grid_spec = pltpu.PrefetchScalarGridSpec(
        num_scalar_prefetch=2,
        grid=(n_t,),
        in_specs=[pl.BlockSpec((1, 1, tm * TOP_K), lambda i, c, p: (i, 0, 0), memory_space=pltpu.SMEM),
                  pl.BlockSpec(memory_space=pl.ANY)],
        out_specs=pl.BlockSpec(memory_space=pl.ANY),
        scratch_shapes=[pltpu.VMEM((3, tm, 1, D_MODEL), F32), pltpu.VMEM((tb, 1, D_MODEL), F32),
                        pltpu.SemaphoreType.DMA((3,)), pltpu.SemaphoreType.DMA((2,)),
                        pltpu.SemaphoreType.DMA(())],
    )
    return pl.pallas_call(
        functools.partial(_dispatch_kernel, tm=tm, tb=tb, n_blocks=n_rows // tb, n_steps=n_t),
        grid_spec=grid_spec,
        out_shape=jax.ShapeDtypeStruct((n_rows, 1, D_MODEL), F32),
        compiler_params=_params(("arbitrary",)),
        name="moe_dispatch",
    )(counts, pstart, dest3, h)


def _staged_expert_kernel(be_ref, nvb_ref, lo_ref, hi_ref, kick_ref, nxt_ref, par_ref, x_ref, *rest,
                          n_mats, compute, row_major_out, n_steps, tb):
    w_refs = rest[:n_mats]
    b_refs = rest[n_mats:2 * n_mats]
    o_ref = rest[2 * n_mats]
    wb, stage, sems, dense = rest[2 * n_mats + 1:2 * n_mats + 5]
    i = pl.program_id(0)
    n_chunks = wb.shape[2] // MOE_W_CHUNK
    if row_major_out:
        ybuf, osems = rest[2 * n_mats + 5:]
        oslot = i % 2

        def out_copy(step, s):
            rows = pl.ds(pl.multiple_of(step * tb, tb), tb)
            return pltpu.make_async_copy(ybuf.at[s], o_ref.at[rows, 0, :], osems.at[s])

        @pl.when(i >= 2)
        def _():
            out_copy(0, oslot).wait()

        target = ybuf.at[oslot]
    else:
        target = o_ref

    def chunk_rows(c):
        return pl.ds(pl.multiple_of(c * MOE_W_CHUNK, MOE_W_CHUNK), MOE_W_CHUNK)

    def copies(e, c):
        slot = c % 2
        return [pltpu.make_async_copy(w.at[e, chunk_rows(c), :], stage.at[slot, mi], sems.at[slot])
                for mi, w in enumerate(w_refs)]

    def start(e, c):
        for cp in copies(e, c):
            cp.start()

    def finish_range(e, lo, hi, dst):
        def body(c, carry):
            for cp in copies(e, c):
                cp.wait()
            for mi in range(n_mats):
                wb[dst, mi, chunk_rows(c), :] = stage[c % 2, mi].astype(BF16)

            @pl.when(c + 2 < n_chunks)
            def _():
                start(e, c + 2)

            return carry

        lax.fori_loop(lo, hi, body, 0)

    cur = par_ref[i]

    @pl.when(i == 0)
    def _():
        start(be_ref[0], 0)
        start(be_ref[0], 1)
        finish_range(be_ref[0], 0, n_chunks, cur)

    @pl.when(kick_ref[i] == 1)
    def _():
        start(nxt_ref[i], 0)
        start(nxt_ref[i], 1)

    @pl.when(i < nvb_ref[0])
    def _():
        compute(x_ref, [wb.at[cur, mi] for mi in range(n_mats)], b_refs, target, dense)

    @pl.when(i >= nvb_ref[0])
    def _():
        target[...] = jnp.zeros(target.shape, target.dtype)

    if row_major_out:
        out_copy(i, oslot).start()

    finish_range(nxt_ref[i], lo_ref[i], hi_ref[i], 1 - cur)

    if row_major_out:
        @pl.when(i == n_steps - 1)
        def _():
            if n_steps >= 2:
                out_copy(0, 1 - oslot).wait()
            out_copy(0, oslot).wait()


def _up_compute(x_ref, w, b_refs, a_ref, dense):
    dense[...] = x_ref[:, 0, :]
    xb = dense[...].astype(BF16)
    half = a_ref.shape[1] // 2
    for n in range(2):
        cols = slice(n * half, (n + 1) * half)
        gate = jnp.minimum(_dot(xb, w[0][:, cols]) + b_refs[0][0, :, cols], SWIGLU_LIMIT)
        up = jnp.clip(_dot(xb, w[1][:, cols]) + b_refs[1][0, :, cols], -SWIGLU_LIMIT, SWIGLU_LIMIT)
        a_ref[:, cols] = (gate * jax.nn.sigmoid(SWIGLU_ALPHA * gate) * (up + 1.0)).astype(a_ref.dtype)


def _down_compute(a_ref, w, b_refs, y_ref, dense):
    ab = a_ref[...]
    half = y_ref.shape[1] // 2
    for n in range(2):
        cols = slice(n * half, (n + 1) * half)
        y_ref[:, cols] = _dot(ab, w[0][:, cols]) + b_refs[0][0, :, cols]


def _staged_expert_call(x_rows, meta, weights, biases, out_dtype, compute, name, tb, row_major_out=False):
    n_rows = x_rows.shape[0]
    k, n_out = weights[0].shape[1:]
    n_mats = len(weights)
    x_block = (tb,) + x_rows.shape[1:]
    rows_map = lambda i, be, nv, *_: (jnp.minimum(i, nv[0] - 1),) + (0,) * (len(x_block) - 1)
    bias_spec = pl.BlockSpec((1, 1, n_out), lambda i, be, *_: (be[i], 0, 0))
    scratch = [pltpu.VMEM((2, n_mats, k, n_out), BF16),
               pltpu.VMEM((2, n_mats, MOE_W_CHUNK, n_out), F32),
               pltpu.SemaphoreType.DMA((2,)),
               pltpu.VMEM((tb, k) if len(x_block) == 3 else (8, LANES), F32)]
    if row_major_out:
        out_shape, out_spec = (n_rows, 1, n_out), pl.BlockSpec(memory_space=pl.ANY)
        scratch += [pltpu.VMEM((2, tb, n_out), out_dtype), pltpu.SemaphoreType.DMA((2,))]
    else:
        out_shape, out_spec = (n_rows, n_out), pl.BlockSpec((tb, n_out), lambda i, *_: (i, 0))
    grid_spec = pltpu.PrefetchScalarGridSpec(
        num_scalar_prefetch=len(meta),
        grid=(n_rows // tb,),
        in_specs=([pl.BlockSpec(x_block, rows_map)] + [pl.BlockSpec(memory_space=pl.ANY)] * n_mats
                  + [bias_spec] * n_mats),
        out_specs=out_spec,
        scratch_shapes=scratch,
    )
    return pl.pallas_call(
        functools.partial(_staged_expert_kernel, n_mats=n_mats, compute=compute,
                          row_major_out=row_major_out, n_steps=n_rows // tb, tb=tb),
        grid_spec=grid_spec,
        out_shape=jax.ShapeDtypeStruct(out_shape, out_dtype),
        compiler_params=_params(("arbitrary",)),
        name=name,
    )(*meta, x_rows, *weights, *[b.reshape(N_EXPERTS, 1, n_out) for b in biases])


def _expert_schedule(blk_e, nvb, padded, pstart, tb, n_chunks):
    nb = blk_e.shape[0]
    ids = jnp.arange(N_EXPERTS, dtype=I32)
    n_blk = padded // tb
    has = n_blk > 0
    later = jnp.where(has[None, :] & (ids[None, :] > ids[:, None]), ids[None, :], N_EXPERTS)
    nxt_e = jnp.min(later, axis=1)
    nxt_e = jnp.where(nxt_e == N_EXPERTS, -1, nxt_e)
    par_e = jnp.maximum(jnp.cumsum(has.astype(I32)) - 1, 0) % 2
    bi = jnp.arange(nb, dtype=I32)
    rank = bi - (pstart // tb)[blk_e]
    n_e = jnp.maximum(n_blk[blk_e], 1)
    active = (bi < nvb[0]) & (nxt_e[blk_e] >= 0)
    lo = jnp.where(active, n_chunks * rank // n_e, 0)
    hi = jnp.where(active, n_chunks * (rank + 1) // n_e, 0)
    kick = (active & (rank == 0)).astype(I32)
    to_i32 = lambda a: a.astype(I32)
    return tuple(map(to_i32, (blk_e, nvb, lo, hi, kick, nxt_e[blk_e], par_e[blk_e])))


def _combine_kernel(dcur_ref, dnxt_ref, y_ref, h_ref, gate_ref, g_ref, b_ref, yp_ref, ys_ref,
                    buf, sems, *, tm, n_tiles, n_prompt_tiles):
    i = pl.program_id(0)
    slot = i % 2

    def row_copy(dref, t, kk, sl):
        r = dref[0, 0, t * TOP_K + kk]
        return pltpu.make_async_copy(y_ref.at[pl.ds(r, 1)], buf.at[sl, kk, pl.ds(t, 1)], sems.at[sl])

    def issue(dref, sl):
        def body(t, carry):
            for kk in range(TOP_K):
                row_copy(dref, t, kk, sl).start(priority=kk % 2)
            return carry
        lax.fori_loop(0, tm, body, 0)

    @pl.when(i == 0)
    def _():
        issue(dcur_ref, 0)

    @pl.when(i + 1 < n_tiles)
    def _():
        issue(dnxt_ref, 1 - slot)

    def drain(t, carry):
        for kk in range(TOP_K):
            row_copy(dcur_ref, t, kk, slot).wait()
        return carry

    lax.fori_loop(0, tm, drain, 0)
    gate = gate_ref[...]
    f = jnp.zeros((tm, D_MODEL), F32)
    for kk in range(TOP_K):
        f = f + buf[slot, kk, :, 0, :] * gate[:, TOP_K + kk:TOP_K + kk + 1]
    y = _layer_norm(DEEPNORM_ALPHA * h_ref[...] + f, g_ref[...], b_ref[...])

    @pl.when(i < n_prompt_tiles)
    def _():
        yp_ref[...] = y

    @pl.when(i >= n_prompt_tiles)
    def _():
        ys_ref[...] = y


def _combine(y_rows, h, route, dest, g, b, n_prompt, n_sample, tm):
    n_p = n_prompt // tm
    n_tiles = n_p + n_sample // tm
    dest3 = dest.reshape(-1, 1, tm * TOP_K)
    smem = functools.partial(pl.BlockSpec, (1, 1, tm * TOP_K), memory_space=pltpu.SMEM)
    c2 = lambda i: (0, 0)
    return pl.pallas_call(
        functools.partial(_combine_kernel, tm=tm, n_tiles=n_tiles, n_prompt_tiles=n_p),
        grid=(n_tiles,),
        in_specs=[smem(lambda i: (i, 0, 0)),
                  smem(lambda i: (jnp.minimum(i + 1, n_tiles - 1), 0, 0)),
                  pl.BlockSpec(memory_space=pl.ANY),
                  pl.BlockSpec((tm, D_MODEL), lambda i: (i, 0)),
                  pl.BlockSpec((tm, LANES), lambda i: (i, 0)),
                  pl.BlockSpec(g.shape, c2), pl.BlockSpec(b.shape, c2)],
        out_specs=[pl.BlockSpec((tm, D_MODEL), lambda i: (jnp.minimum(i, n_p - 1), 0)),
                   pl.BlockSpec((tm, D_MODEL), lambda i: (jnp.maximum(i - n_p, 0), 0))],
        out_shape=[jax.ShapeDtypeStruct((n_prompt, D_MODEL), F32),
                   jax.ShapeDtypeStruct((n_sample, D_MODEL), F32)],
        scratch_shapes=[pltpu.VMEM((2, TOP_K, tm, 1, D_MODEL), F32), pltpu.SemaphoreType.DMA((2,))],
        compiler_params=_params(("arbitrary",)),
        name="moe_combine_ln",
    )(dest3, dest3, y_rows, h, route, g, b)


def _moe_and_norm(h, route, cnt, n_prompt, n_sample, w_exp_gate, b_exp_gate, w_exp_up, b_exp_up,
                  w_exp_down, b_exp_down, ln2_g, ln2_b, tb, out_tm, comb_tm):
    m_tot = h.shape[0]
    top_e = route[:, 0:TOP_K].astype(I32)
    rank = route[:, 2 * TOP_K:3 * TOP_K].astype(I32)
    counts = cnt[0, 0:N_EXPERTS].astype(I32)
    padded = (counts + tb - 1) // tb * tb
    pend = jnp.cumsum(padded)
    pstart = pend - padded
    dest = (pstart[top_e] + rank).reshape(-1)
    nb = (m_tot * TOP_K + N_EXPERTS * (tb - 1) + tb - 1) // tb
    nvb = (pend[-1:] // tb).astype(I32)
    blk_start = jnp.arange(nb, dtype=I32) * tb
    blk_e = jnp.minimum(jnp.sum((blk_start[:, None] >= pend[None, :]).astype(I32), axis=1), N_EXPERTS - 1)
    meta = _expert_schedule(blk_e, nvb, padded, pstart, tb, D_MODEL // MOE_W_CHUNK)
    x_rows = _dispatch(h, dest, counts, pstart.astype(I32), nb * tb, out_tm, tb)
    a_rows = _staged_expert_call(x_rows, meta, (w_exp_gate, w_exp_up), (b_exp_gate, b_exp_up), BF16,
                                 _up_compute, "moe_up", tb)
    y_rows = _staged_expert_call(a_rows, meta, (w_exp_down,), (b_exp_down,), F32,
                                 _down_compute, "moe_down", tb, row_major_out=True)
    return _combine(y_rows, h, route, dest, ln2_g, ln2_b, n_prompt, n_sample, comb_tm)


def _split_w_in(w_in):
    sizes = (GLA_QK, GLA_QK, GLA_WIDTH, GLA_GATE_RANK, GLA_WIDTH, DSWA_WIDTH, DSWA_WIDTH, DSWA_WIDTH)
    o = np.cumsum((0,) + sizes)
    wg = jnp.concatenate([w_in[:, o[0]:o[3]], w_in[:, o[4]:o[5]]], axis=1).astype(BF16)
    wlr = jnp.pad(w_in[:, o[3]:o[4]], ((0, 0), (0, LANES - GLA_GATE_RANK))).astype(BF16)
    wd = w_in[:, o[5]:o[8]].astype(BF16)
    return wg, wlr, wd


def kernel(x_prompt, x_sample, state_gla, cache_dswa_k, cache_dswa_v, w_in, w_gla_alpha, b_gla_alpha,
           gla_norm_w, w_o, ln1_g, ln1_b, w_router, b_router, w_exp_gate, b_exp_gate, w_exp_up,
           b_exp_up, w_exp_down, b_exp_down, ln2_g, ln2_b):
    bp, sp, _ = x_prompt.shape
    bs, ls, _ = x_sample.shape
    n_prompt, n_sample = bp * sp, bs * ls
    n_past = cache_dswa_k.shape[1]

    wg, wlr, wd = _split_w_in(w_in)
    wa = jnp.pad(w_gla_alpha, ((0, LANES - GLA_GATE_RANK), (0, 0))).astype(BF16)
    ba = b_gla_alpha.reshape(1, GLA_QK)
    nw = gla_norm_w.reshape(1, GLA_WIDTH)
    xp = x_prompt.reshape(n_prompt, D_MODEL)
    xs = x_sample.reshape(n_sample, D_MODEL)

    q, k, v, r, g = _proj_gla(xp, wg, wlr, wa, ba, PROJ_TM)
    og_p, state_p = _gla(q, k, g, v, r, nw, jnp.zeros((bp, GLA_HEADS, GLA_DK, GLA_DV), F32),
                         bp, sp, GLA_ROWS, GLA_STEP_HEADS)
    dqb, dk, dv, dkb, dvb = _proj_dswa(xp, wd, PROJ_TM)
    od_p = _dswa_prompt(dqb, dkb, dvb, bp, sp, ATT_T, ATT_HEADS)

    qs, ks, vs, rs, gs = _proj_gla(xs, wg, wlr, wa, ba, n_sample)
    lpad = GLA_CHUNK - ls
    pad_rows = lambda a: jnp.pad(a.reshape(bs, ls, -1), ((0, 0), (0, lpad), (0, 0))).reshape(bs * GLA_CHUNK, -1)
    og_s_pad, state_s = _gla(pad_rows(qs), pad_rows(ks), pad_rows(gs), pad_rows(vs), pad_rows(rs), nw,
                             state_gla, bs, GLA_CHUNK, GLA_CHUNK, GLA_STEP_HEADS)
    og_s = og_s_pad.reshape(bs, GLA_CHUNK, GLA_WIDTH)[:, :ls].reshape(n_sample, GLA_WIDTH)
    tqb, tk, tv, _, _ = _proj_dswa(xs, wd, n_sample)
    heads4 = lambda a: a.reshape(bs, ls, DSWA_HEADS, DSWA_HEAD_DIM)
    od_s = _dswa_sample(heads4(tqb), heads4(tk), heads4(tv), cache_dswa_k, cache_dswa_v)
    od_s = od_s.reshape(n_sample, DSWA_WIDTH).astype(BF16)

    spad = ((0, OUT_TM - n_sample), (0, 0))
    wr = jnp.pad(w_router, ((0, 0), (0, LANES - N_EXPERTS))).astype(BF16)
    br = jnp.pad(b_router, (0, LANES - N_EXPERTS), constant_values=NEG_BIG).reshape(1, LANES)
    h, route, cnt = _outproj(og_p, od_p, xp, jnp.pad(og_s, spad), jnp.pad(od_s, spad), jnp.pad(xs, spad),
                             w_o.astype(BF16), ln1_g.reshape(1, D_MODEL), ln1_b.reshape(1, D_MODEL),
                             wr, br, OUT_TM)
    y_p, y_s = _moe_and_norm(h, route, cnt, n_prompt, n_sample, w_exp_gate, b_exp_gate, w_exp_up,
                             b_exp_up, w_exp_down, b_exp_down, ln2_g.reshape(1, D_MODEL),
                             ln2_b.reshape(1, D_MODEL), MOE_TB, OUT_TM, COMB_TM)

    return (y_p.reshape(bp, sp, D_MODEL), y_s.reshape(bs, ls, D_MODEL), state_p,
            dk.reshape(bp, sp, DSWA_HEADS, DSWA_HEAD_DIM), dv.reshape(bp, sp, DSWA_HEADS, DSWA_HEAD_DIM),
            state_s, tk.reshape(bs, ls, DSWA_HEADS, DSWA_HEAD_DIM),
            tv.reshape(bs, ls, DSWA_HEADS, DSWA_HEAD_DIM))
```
